```python
import math
import jax, jax.numpy as jnp
from jax import lax
import numpy as np

D_MODEL = 2048
BATCH = 4
SEQ = 4096
DEPTH = 1

MIX_WIDTH = D_MODEL
POOL_WIDTH = MIX_WIDTH // 2
POOL_WINDOWS = (2, 4, 8, 16)
POOL_GROUPS = len(POOL_WINDOWS)
POOL_GROUP_WIDTH = POOL_WIDTH // POOL_GROUPS
ATTN_WIDTH = MIX_WIDTH - POOL_WIDTH
HEAD_DIM = 64
N_Q_HEADS = ATTN_WIDTH // HEAD_DIM
N_KV_HEADS = max(1, N_Q_HEADS // 8)
Q_PER_KV = N_Q_HEADS // N_KV_HEADS
KV_WIDTH = N_KV_HEADS * HEAD_DIM
IN_WIDTH = POOL_WIDTH + ATTN_WIDTH + 2 * KV_WIDTH
WINDOW = 128
ATTN_BLOCK = 128
ROT_DIM = HEAD_DIM // 4
ROPE_THETA = 500000.0
N_EXPERTS = 256
TOP_K = 8
N_EXPERT_GROUPS = 8
TOPK_GROUPS = 4
EXPERT_DIM = D_MODEL // 4
SHARED_DIM = EXPERT_DIM
ROUTED_SCALE = 2.5
MOE_BLOCK = 128
LN_EPS = 1e-5
DEEPNORM_ALPHA = (2.0 * DEPTH) ** 0.25
DEEPNORM_BETA = (8.0 * DEPTH) ** -0.25

kernel_name = "hymba_pool_swa_sink_moe_deepnorm"


def layer_norm(x, g, b):
    xf = x.astype(jnp.float32)
    mu = xf.mean(-1, keepdims=True)
    var = jnp.square(xf - mu).mean(-1, keepdims=True)
    y = (xf - mu) * lax.rsqrt(var + LN_EPS) * g.astype(jnp.float32) + b.astype(jnp.float32)
    return y.astype(x.dtype)


def partial_rope(t, pos):
    half = ROT_DIM // 2
    inv_freq = ROPE_THETA ** (-jnp.arange(half, dtype=jnp.float32) / half)
    ang = pos.astype(jnp.float32)[:, None] * inv_freq[None, :]
    cos = jnp.cos(ang)[None, :, None, :]
    sin = jnp.sin(ang)[None, :, None, :]
    tr = t[..., :ROT_DIM].astype(jnp.float32)
    t1, t2 = tr[..., :half], tr[..., half:]
    rot = jnp.concatenate([t1 * cos - t2 * sin, t2 * cos + t1 * sin], axis=-1).astype(t.dtype)
    return jnp.concatenate([rot, t[..., ROT_DIM:]], axis=-1)


def multiscale_pool(u, w_pool, pool_scale):
    S_ = u.shape[1]
    uf = u.astype(jnp.float32)
    cs = jnp.pad(jnp.cumsum(uf, axis=1), ((0, 0), (1, 0), (0, 0)))
    t = jnp.arange(1, S_ + 1, dtype=jnp.float32)
    groups = []
    for g, w in enumerate(POOL_WINDOWS):
        sl = slice(g * POOL_GROUP_WIDTH, (g + 1) * POOL_GROUP_WIDTH)
        hi = cs[:, 1:, sl]
        lo = jnp.pad(cs[:, :S_ + 1 - w, sl], ((0, 0), (w - 1, 0), (0, 0)))
        count = jnp.minimum(t, float(w))[None, :, None]
        groups.append((hi - lo) / count - uf[:, :, sl])
    pooled = jnp.stack(groups, axis=2).astype(u.dtype)
    mixed = jnp.einsum('bsgc,gcd->bsgd', pooled, w_pool)
    return mixed.reshape(u.shape) * pool_scale


def sliding_window_attention(q, k, v, sinks):
    B_, S_ = q.shape[:2]
    nb = S_ // ATTN_BLOCK
    qb = q.reshape(B_, nb, ATTN_BLOCK, N_KV_HEADS, Q_PER_KV, HEAD_DIM)
    kb = k.reshape(B_, nb, ATTN_BLOCK, N_KV_HEADS, HEAD_DIM)
    vb = v.reshape(B_, nb, ATTN_BLOCK, N_KV_HEADS, HEAD_DIM)

    def with_prev(t):
        prev = jnp.pad(t[:, :-1], ((0, 0), (1, 0), (0, 0), (0, 0), (0, 0)))
        return jnp.concatenate([prev, t], axis=2)

    kk, vv = with_prev(kb), with_prev(vb)
    scores = jnp.einsum('bnqhgd,bnkhd->bnhgqk', qb, kk,
                        preferred_element_type=jnp.float32) * (HEAD_DIM ** -0.5)
    blk = jnp.arange(nb)[:, None] * ATTN_BLOCK
    qpos = blk + jnp.arange(ATTN_BLOCK)[None, :]
    kpos = blk - ATTN_BLOCK + jnp.arange(2 * ATTN_BLOCK)[None, :]
    delta = qpos[:, :, None] - kpos[:, None, :]
    allowed = (delta >= 0) & (delta < WINDOW) & (kpos[:, None, :] >= 0)
    scores = jnp.where(allowed[None, :, None, None], scores, -jnp.inf)
    sink = sinks.astype(jnp.float32).reshape(N_KV_HEADS, Q_PER_KV)[None, None, :, :, None, None]
    m = jnp.maximum(scores.max(-1, keepdims=True), sink)
    p = jnp.exp(scores - m)
    denom = p.sum(-1, keepdims=True) + jnp.exp(sink - m)
    probs = (p / denom).astype(v.dtype)
    out = jnp.einsum('bnhgqk,bnkhd->bnqhgd', probs, vv)
    return out.reshape(B_, S_, N_Q_HEADS * HEAD_DIM)


def route(xf, w_router, router_bias):
    T = xf.shape[0]
    scores = jax.nn.sigmoid(jnp.dot(xf, w_router, preferred_element_type=jnp.float32))
    biased = scores + router_bias.astype(jnp.float32)
    grp = biased.reshape(T, N_EXPERT_GROUPS, N_EXPERTS // N_EXPERT_GROUPS)
    grp_score = lax.top_k(grp, 2)[0].sum(-1)
    _, grp_idx = lax.top_k(grp_score, TOPK_GROUPS)
    grp_keep = jax.nn.one_hot(grp_idx, N_EXPERT_GROUPS, dtype=jnp.float32).sum(1) > 0
    expert_keep = jnp.repeat(grp_keep, N_EXPERTS // N_EXPERT_GROUPS, axis=1)
    _, eidx = lax.top_k(jnp.where(expert_keep, biased, -jnp.inf), TOP_K)
    gates = jnp.take_along_axis(scores, eidx, axis=-1)
    gates = gates / gates.sum(-1, keepdims=True) * ROUTED_SCALE
    return eidx, gates


def routed_experts(xf, eidx, gates, w_gate, w_up, w_down):
    T = xf.shape[0]
    TK = T * TOP_K
    n_blocks = (TK + N_EXPERTS * (MOE_BLOCK - 1)) // MOE_BLOCK
    e_flat = eidx.reshape(TK)
    tok_flat = jnp.repeat(jnp.arange(T, dtype=jnp.int32), TOP_K)
    g_flat = gates.reshape(TK)
    order = jnp.argsort(e_flat, stable=True)
    e_sorted = e_flat[order]
    sizes = jnp.bincount(e_flat, length=N_EXPERTS)
    padded = (sizes + MOE_BLOCK - 1) // MOE_BLOCK * MOE_BLOCK
    start = jnp.cumsum(sizes) - sizes
    pad_end = jnp.cumsum(padded)
    pad_start = pad_end - padded
    dest = pad_start[e_sorted] + jnp.arange(TK, dtype=jnp.int32) - start[e_sorted]
    n_rows = n_blocks * MOE_BLOCK
    tok_buf = jnp.zeros((n_rows,), jnp.int32).at[dest].set(tok_flat[order])
    gate_buf = jnp.zeros((n_rows,), jnp.float32).at[dest].set(g_flat[order])
    block_expert = jnp.minimum(
        jnp.searchsorted(pad_end, jnp.arange(n_blocks) * MOE_BLOCK, side='right'), N_EXPERTS - 1)

    def step(acc, blk):
        tok, g, e = blk
        xb = xf[tok]
        h = jax.nn.silu(xb @ w_gate[e]) * (xb @ w_up[e])
        y = (h @ w_down[e]).astype(jnp.float32) * g[:, None]
        return acc.at[tok].add(y), None

    acc, _ = lax.scan(step, jnp.zeros((T, D_MODEL), jnp.float32),
                      (tok_buf.reshape(n_blocks, MOE_BLOCK), gate_buf.reshape(n_blocks, MOE_BLOCK),
                       block_expert))
    return acc.astype(xf.dtype)


def hybrid_layer(x, w_in, w_pool, pool_scale, attn_sinks, w_o, ln1_g, ln1_b, w_router,
                 router_bias, w_exp_gate, w_exp_up, w_exp_down, w_sh_gate, w_sh_up, w_sh_down,
                 ln2_g, ln2_b):
    B_, S_, _ = x.shape
    pos = jnp.arange(S_)
    proj = x @ w_in
    c1 = POOL_WIDTH
    c2 = c1 + ATTN_WIDTH
    c3 = c2 + KV_WIDTH
    u, q, k, v = proj[..., :c1], proj[..., c1:c2], proj[..., c2:c3], proj[..., c3:]
    pool_out = multiscale_pool(u, w_pool, pool_scale)
    q = partial_rope(q.reshape(B_, S_, N_Q_HEADS, HEAD_DIM), pos)
    k = partial_rope(k.reshape(B_, S_, N_KV_HEADS, HEAD_DIM), pos)
    v = v.reshape(B_, S_, N_KV_HEADS, HEAD_DIM)
    attn_out = sliding_window_attention(q, k, v, attn_sinks)
    mix = jnp.concatenate([pool_out, attn_out], axis=-1) @ w_o
    x = layer_norm(DEEPNORM_ALPHA * x + mix, ln1_g, ln1_b)
    xf = x.reshape(B_ * S_, D_MODEL)
    eidx, gates = route(xf, w_router, router_bias)
    routed = routed_experts(xf, eidx, gates, w_exp_gate, w_exp_up, w_exp_down)
    shared = (jax.nn.silu(xf @ w_sh_gate) * (xf @ w_sh_up)) @ w_sh_down
    moe = (routed + shared).reshape(B_, S_, D_MODEL)
    return layer_norm(DEEPNORM_ALPHA * x + moe, ln2_g, ln2_b)


def setup_inputs(seed: int = 0) -> dict:
    key = jax.random.key(seed)
    ks = jax.random.split(key, 18)
    L = DEPTH

    def normal(k, shape, scale):
        return jax.random.normal(k, shape, jnp.float32) * scale

    beta = DEEPNORM_BETA
    w_in = normal(ks[1], (L, D_MODEL, IN_WIDTH), D_MODEL ** -0.5)
    v_col_scale = jnp.concatenate([jnp.ones((IN_WIDTH - KV_WIDTH,), jnp.float32),
                                   jnp.full((KV_WIDTH,), beta, jnp.float32)])
    return {
        "x": normal(ks[0], (BATCH, SEQ, D_MODEL), 1.0),
        "w_in": w_in * v_col_scale,
        "w_pool": normal(ks[2], (L, POOL_GROUPS, POOL_GROUP_WIDTH, POOL_GROUP_WIDTH), POOL_GROUP_WIDTH ** -0.5),
        "pool_scale": 1.0 + normal(ks[3], (L, POOL_WIDTH), 0.02),
        "attn_sinks": normal(ks[4], (L, N_Q_HEADS), 0.5),
        "w_o": normal(ks[5], (L, MIX_WIDTH, D_MODEL), MIX_WIDTH ** -0.5 * beta),
        "ln1_g": 1.0 + normal(ks[6], (L, D_MODEL), 0.02),
        "ln1_b": normal(ks[7], (L, D_MODEL), 0.02),
        "w_router": normal(ks[8], (L, D_MODEL, N_EXPERTS), D_MODEL ** -0.5),
        "router_bias": normal(ks[9], (L, N_EXPERTS), 0.01),
        "w_exp_gate": normal(ks[10], (L, N_EXPERTS, D_MODEL, EXPERT_DIM), D_MODEL ** -0.5),
        "w_exp_up": normal(ks[11], (L, N_EXPERTS, D_MODEL, EXPERT_DIM), D_MODEL ** -0.5),
        "w_exp_down": normal(ks[12], (L, N_EXPERTS, EXPERT_DIM, D_MODEL), EXPERT_DIM ** -0.5 * beta),
        "w_sh_gate": normal(ks[13], (L, D_MODEL, SHARED_DIM), D_MODEL ** -0.5),
        "w_sh_up": normal(ks[14], (L, D_MODEL, SHARED_DIM), D_MODEL ** -0.5),
        "w_sh_down": normal(ks[15], (L, SHARED_DIM, D_MODEL), SHARED_DIM ** -0.5 * beta),
        "ln2_g": 1.0 + normal(ks[16], (L, D_MODEL), 0.02),
        "ln2_b": normal(ks[17], (L, D_MODEL), 0.02),
    }


def reference(x, w_in, w_pool, pool_scale, attn_sinks, w_o, ln1_g, ln1_b, w_router, router_bias,
              w_exp_gate, w_exp_up, w_exp_down, w_sh_gate, w_sh_up, w_sh_down, ln2_g, ln2_b):
    for l in range(DEPTH):
        x = hybrid_layer(x, w_in[l], w_pool[l], pool_scale[l], attn_sinks[l], w_o[l], ln1_g[l],
                         ln1_b[l], w_router[l], router_bias[l], w_exp_gate[l], w_exp_up[l],
                         w_exp_down[l], w_sh_gate[l], w_sh_up[l], w_sh_down[l], ln2_g[l], ln2_b[l])
    return x
```

```python
import functools

import jax
import jax.numpy as jnp
from jax import lax
from jax.experimental import pallas as pl
from jax.experimental.pallas import tpu as pltpu

F32 = jnp.float32
BF16 = jnp.bfloat16
I32 = jnp.int32
U32 = jnp.uint32

D_MODEL = 2048
POOL_WIDTH = 1024
POOL_WINDOWS = (2, 4, 8, 16)
POOL_GROUP_WIDTH = 256
ATTN_WIDTH = 1024
HEAD_DIM = 64
N_Q_HEADS = 16
N_KV_HEADS = 2
Q_PER_KV = 8
KV_WIDTH = 128
ATTN_BLOCK = 128
ROT_DIM = 16
ROPE_THETA = 500000.0
N_EXPERTS = 256
TOP_K = 8
N_EXPERT_GROUPS = 8
GROUP_SIZE = N_EXPERTS // N_EXPERT_GROUPS
TOPK_GROUPS = 4
EXPERT_DIM = 512
ROUTED_SCALE = 2.5
LN_EPS = 1e-5

LANES = 128
SUBLANES = 8
VMEM_LIMIT_BYTES = 56 * 1024 * 1024

PROJ_ROWS = 512
MIX_ROWS = 512
POST_ROWS = 256
DISPATCH_ROWS = 256
EXPERT_ROWS = 128
COMBINE_ROWS = 128
HALO = 16
PACKED = D_MODEL // 2

NEG_INF = float("-inf")


def _dot(a, b):
    return jnp.dot(a, b, preferred_element_type=F32)


def _dot_nt(a, b):
    return lax.dot_general(a, b, (((1,), (1,)), ((), ())), preferred_element_type=F32)


def _layer_norm(y, g, b):
    mu = jnp.mean(y, axis=-1, keepdims=True)
    yc = y - mu
    var = jnp.mean(yc * yc, axis=-1, keepdims=True)
    return yc * lax.rsqrt(var + LN_EPS) * g + b


def _in_proj_kernel(x_ref, w_ref, cos_ref, sneg_ref, spos_ref, u_ref, q_ref, k_ref, v_ref):
    xb = x_ref[...].astype(BF16)
    c = cos_ref[...]
    sn = sneg_ref[...]
    sp = spos_ref[...]

    def rope(t):
        return t * c + pltpu.roll(t, LANES - ROT_DIM // 2, 1) * sn + pltpu.roll(t, ROT_DIM // 2, 1) * sp

    u_ref[...] = _dot(xb, w_ref[:, :POOL_WIDTH])
    qf = _dot(xb, w_ref[:, POOL_WIDTH:POOL_WIDTH + ATTN_WIDTH])
    for j in range(ATTN_WIDTH // LANES):
        q_ref[:, j * LANES:(j + 1) * LANES] = rope(qf[:, j * LANES:(j + 1) * LANES]).astype(BF16)
    kv = _dot(xb, w_ref[:, POOL_WIDTH + ATTN_WIDTH:])
    for j in range(N_KV_HEADS):
        k_ref[:, j * LANES:(j + 1) * LANES] = rope(kv[:, j * LANES:(j + 1) * LANES]).astype(BF16)
    v_ref[...] = kv[:, N_KV_HEADS * LANES:].astype(BF16)


def _in_proj(x2, w_ext, cos_t, sneg_t, spos_t, seq):
    t = x2.shape[0]
    rows = PROJ_ROWS
    n_pos = seq // rows
    wcols = w_ext.shape[1]
    kvw = N_KV_HEADS * LANES
    pos_map = lambda i: (i % n_pos, 0)
    return pl.pallas_call(
        _in_proj_kernel,
        grid=(t // rows,),
        in_specs=[
            pl.BlockSpec((rows, D_MODEL), lambda i: (i, 0)),
            pl.BlockSpec((D_MODEL, wcols), lambda i: (0, 0)),
            pl.BlockSpec((rows, LANES), pos_map),
            pl.BlockSpec((rows, LANES), pos_map),
            pl.BlockSpec((rows, LANES), pos_map),
        ],
        out_specs=[
            pl.BlockSpec((rows, POOL_WIDTH), lambda i: (i, 0)),
            pl.BlockSpec((rows, ATTN_WIDTH), lambda i: (i, 0)),
            pl.BlockSpec((rows, kvw), lambda i: (i, 0)),
            pl.BlockSpec((rows, kvw), lambda i: (i, 0)),
        ],
        out_shape=[
            jax.ShapeDtypeStruct((t, POOL_WIDTH), F32),
            jax.ShapeDtypeStruct((t, ATTN_WIDTH), BF16),
            jax.ShapeDtypeStruct((t, kvw), BF16),
            jax.ShapeDtypeStruct((t, kvw), BF16),
        ],
        compiler_params=pltpu.CompilerParams(
            dimension_semantics=("arbitrary",), vmem_limit_bytes=VMEM_LIMIT_BYTES),
        name="in_proj",
    )(x2, w_ext, cos_t, sneg_t, spos_t)


def _mixer_kernel(sinks_ref, u_ref, uh_ref, q_ref, k_ref, kp_ref, v_ref, vp_ref, wp_ref, ps_ref,
                  o_ref, ubuf, kbuf, vbuf):
    j = pl.program_id(1)
    rows = u_ref.shape[0]
    first = j == 0

    ubuf[HALO:HALO + rows, :] = u_ref[...]

    @pl.when(first)
    def _():
        ubuf[0:HALO, :] = jnp.zeros((HALO, POOL_WIDTH), F32)

    @pl.when(jnp.logical_not(first))
    def _():
        ubuf[0:HALO, :] = uh_ref[...]

    pos = j * rows + lax.broadcasted_iota(I32, (rows, POOL_GROUP_WIDTH), 0)
    for g, w in enumerate(POOL_WINDOWS):
        cs = slice(g * POOL_GROUP_WIDTH, (g + 1) * POOL_GROUP_WIDTH)
        cur = ubuf[HALO:HALO + rows, cs]
        acc = cur
        for d in range(1, w):
            acc = acc + ubuf[HALO - d:HALO - d + rows, cs]
        cnt = jnp.minimum(pos + 1, w).astype(F32)
        pooled = acc / cnt - cur
        mixed = _dot(pooled.astype(BF16), wp_ref[g]) * ps_ref[:, cs]
        o_ref[:, cs] = mixed.astype(BF16)

    kbuf[ATTN_BLOCK:, :] = k_ref[...]
    vbuf[ATTN_BLOCK:, :] = v_ref[...]

    @pl.when(first)
    def _():
        kbuf[0:ATTN_BLOCK, :] = jnp.zeros((ATTN_BLOCK, kbuf.shape[1]), BF16)
        vbuf[0:ATTN_BLOCK, :] = jnp.zeros((ATTN_BLOCK, vbuf.shape[1]), BF16)

    @pl.when(jnp.logical_not(first))
    def _():
        kbuf[0:ATTN_BLOCK, :] = kp_ref[...]
        vbuf[0:ATTN_BLOCK, :] = vp_ref[...]

    qrow = lax.broadcasted_iota(I32, (ATTN_BLOCK, 2 * ATTN_BLOCK), 0)
    kcol = lax.broadcasted_iota(I32, (ATTN_BLOCK, 2 * ATTN_BLOCK), 1)
    in_window = jnp.logical_and(kcol > qrow, kcol <= qrow + ATTN_BLOCK)
    lane = lax.broadcasted_iota(I32, (ATTN_BLOCK, LANES), 1)
    low_half = lane < HEAD_DIM
    scale = HEAD_DIM ** -0.5

    def block_body(blk, carry):
        r0 = pl.multiple_of(blk * ATTN_BLOCK, ATTN_BLOCK)
        has_prev = jnp.logical_or(blk > 0, jnp.logical_not(first))
        first_key = jnp.where(has_prev, 0, ATTN_BLOCK)
        allowed = jnp.logical_and(in_window, kcol >= first_key)
        for h in range(N_KV_HEADS):
            k2 = kbuf[pl.ds(r0, 2 * ATTN_BLOCK), h * LANES:(h + 1) * LANES]
            v2 = vbuf[pl.ds(r0, 2 * ATTN_BLOCK), h * LANES:(h + 1) * LANES]
            parts = []
            for g in range(Q_PER_KV):
                head = h * Q_PER_KV + g
                tile = q_ref[pl.ds(r0, ATTN_BLOCK), (head // 2) * LANES:(head // 2 + 1) * LANES]
                keep = low_half if head % 2 == 0 else jnp.logical_not(low_half)
                parts.append(jnp.where(keep, tile, jnp.zeros_like(tile)))
            q8 = jnp.concatenate(parts, axis=0)
            s8 = _dot_nt(q8, k2) * scale
            probs = []
            for g in range(Q_PER_KV):
                sink = sinks_ref[h * Q_PER_KV + g]
                sg = jnp.where(allowed, s8[g * ATTN_BLOCK:(g + 1) * ATTN_BLOCK], NEG_INF)
                m = jnp.maximum(jnp.max(sg, axis=-1, keepdims=True), sink)
                p = jnp.exp(sg - m)
                denom = jnp.sum(p, axis=-1, keepdims=True) + jnp.exp(sink - m)
                probs.append((p / denom).astype(BF16))
            p8 = jnp.concatenate(probs, axis=0)
            o8 = _dot(p8, v2)
            for pair in range(Q_PER_KV // 2):
                lo = o8[(2 * pair) * ATTN_BLOCK:(2 * pair + 1) * ATTN_BLOCK]
                hi = o8[(2 * pair + 1) * ATTN_BLOCK:(2 * pair + 2) * ATTN_BLOCK]
                col = POOL_WIDTH + (h * (Q_PER_KV // 2) + pair) * LANES
                o_ref[pl.ds(r0, ATTN_BLOCK), col:col + LANES] = jnp.where(low_half, lo, hi).astype(BF16)
        return carry

    lax.fori_loop(0, rows // ATTN_BLOCK, block_body, 0)


def _mixer(u, q, kd, vd, wp, pool_scale, sinks, batch, seq):
    t = u.shape[0]
    rows = MIX_ROWS
    nq = seq // rows
    kvw = kd.shape[1]
    cur = lambda b, j: (b * nq + j, 0)
    halo = lambda b, j: (jnp.maximum((b * seq + j * rows) // HALO - 1, 0), 0)
    prev = lambda b, j: (jnp.maximum((b * seq + j * rows) // ATTN_BLOCK - 1, 0), 0)
    grid_spec = pltpu.PrefetchScalarGridSpec(
        num_scalar_prefetch=0,
        grid=(batch, nq),
        in_specs=[
            pl.BlockSpec(memory_space=pltpu.SMEM),
            pl.BlockSpec((rows, POOL_WIDTH), cur),
            pl.BlockSpec((HALO, POOL_WIDTH), halo),
            pl.BlockSpec((rows, ATTN_WIDTH), cur),
            pl.BlockSpec((rows, kvw), cur),
            pl.BlockSpec((ATTN_BLOCK, kvw), prev),
            pl.BlockSpec((rows, kvw), cur),
            pl.BlockSpec((ATTN_BLOCK, kvw), prev),
            pl.BlockSpec((len(POOL_WINDOWS), POOL_GROUP_WIDTH, POOL_GROUP_WIDTH), lambda b, j: (0, 0, 0)),
            pl.BlockSpec((1, POOL_WIDTH), lambda b, j: (0, 0)),
        ],
        out_specs=pl.BlockSpec((rows, D_MODEL), cur),
        scratch_shapes=[
            pltpu.VMEM((HALO + rows, POOL_WIDTH), F32),
            pltpu.VMEM((ATTN_BLOCK + rows, kvw), BF16),
            pltpu.VMEM((ATTN_BLOCK + rows, kvw), BF16),
        ],
    )
    return pl.pallas_call(
        _mixer_kernel,
        grid_spec=grid_spec,
        out_shape=jax.ShapeDtypeStruct((t, D_MODEL), BF16),
        compiler_params=pltpu.CompilerParams(
            dimension_semantics=("arbitrary", "arbitrary"), vmem_limit_bytes=VMEM_LIMIT_BYTES),
        name="mixer",
    )(sinks, u, u, q, kd, kd, vd, vd, wp, pool_scale)


def _post_mix_kernel(alpha, m_ref, x_ref, wo_ref, g1_ref, b1_ref, wsg_ref, wsu_ref, wsd_ref, wr_ref, rb_ref, tri_ref,
                     base_ref, xpk_ref, eidx_ref, gate_ref, rank_ref, cnt_ref, carry):
    i = pl.program_id(0)
    rows = m_ref.shape[0]

    @pl.when(i == 0)
    def _():
        carry[...] = jnp.zeros_like(carry)

    y = alpha * x_ref[...] + _dot(m_ref[...], wo_ref[...])
    x1 = _layer_norm(y, g1_ref[...], b1_ref[...])
    x1b = x1.astype(BF16)

    hid = jax.nn.silu(_dot(x1b, wsg_ref[...])) * _dot(x1b, wsu_ref[...])
    base_ref[...] = alpha * x1 + _dot(hid.astype(BF16), wsd_ref[...])

    bits = lax.bitcast_convert_type(x1b.astype(F32), U32)
    xpk_ref[...] = jnp.bitwise_or(bits[:, PACKED:] & jnp.uint32(0xFFFF0000), bits[:, :PACKED] >> 16)

    scores = jax.nn.sigmoid(_dot_nt(wr_ref[...], x1b))
    biased = scores + rb_ref[...]
    g3 = biased.reshape(N_EXPERT_GROUPS, GROUP_SIZE, rows)
    in_grp = lax.broadcasted_iota(I32, g3.shape, 1).astype(F32)
    top1 = jnp.max(g3, axis=1, keepdims=True)
    first1 = jnp.min(jnp.where(g3 == top1, in_grp, float(GROUP_SIZE)), axis=1, keepdims=True)
    top2 = jnp.max(jnp.where(in_grp == first1, NEG_INF, g3), axis=1, keepdims=True)
    gscore = top1 + top2
    gid = lax.broadcasted_iota(I32, gscore.shape, 0).astype(F32)
    gkeep = jnp.zeros(gscore.shape, F32)
    for _ in range(TOPK_GROUPS):
        m = jnp.max(gscore, axis=0, keepdims=True)
        pick = jnp.min(jnp.where(gscore == m, gid, float(N_EXPERT_GROUPS)), axis=0, keepdims=True)
        sel = gid == pick
        gkeep = jnp.where(sel, 1.0, gkeep)
        gscore = jnp.where(sel, NEG_INF, gscore)
    cand = jnp.where(gkeep > 0.0, g3, NEG_INF).reshape(N_EXPERTS, rows)

    eid = lax.broadcasted_iota(I32, (N_EXPERTS, rows), 0).astype(F32)
    chosen = jnp.zeros((N_EXPERTS, rows), F32)
    picks, pscores = [], []
    for _ in range(TOP_K):
        m = jnp.max(cand, axis=0, keepdims=True)
        pick = jnp.min(jnp.where(cand == m, eid, float(N_EXPERTS)), axis=0, keepdims=True)
        sel = eid == pick
        pscores.append(jnp.sum(jnp.where(sel, scores, 0.0), axis=0, keepdims=True))
        picks.append(pick)
        chosen = jnp.where(sel, 1.0, chosen)
        cand = jnp.where(sel, NEG_INF, cand)
    total = pscores[0]
    for s in pscores[1:]:
        total = total + s

    before = _dot(chosen.astype(BF16), tri_ref[...]) + carry[...]
    ranks = [jnp.sum(jnp.where(eid == p, before, 0.0), axis=0, keepdims=True) for p in picks]
    carry[...] = carry[...] + jnp.sum(chosen, axis=1, keepdims=True)

    eidx_ref[...] = jnp.concatenate(picks, axis=0).astype(I32)
    gate_ref[...] = jnp.concatenate([s / total * ROUTED_SCALE for s in pscores], axis=0)
    rank_ref[...] = jnp.concatenate(ranks, axis=0).astype(I32)
    cnt_ref[...] = carry[...]


def _post_mix(alpha, mixin, x2, wo, g1, b1, wsg, wsu, wsd, wr_t, rbias, tri):
    t = x2.shape[0]
    rows = POST_ROWS
    const2 = lambda i: (0, 0)
    full = lambda a: pl.BlockSpec(a.shape, const2, pipeline_mode=pl.Buffered(1))
    return pl.pallas_call(
        functools.partial(_post_mix_kernel, alpha),
        grid=(t // rows,),
        in_specs=[
            pl.BlockSpec((rows, D_MODEL), lambda i: (i, 0)),
            pl.BlockSpec((rows, D_MODEL), lambda i: (i, 0)),
            full(wo), full(g1), full(b1), full(wsg), full(wsu), full(wsd), full(wr_t), full(rbias), full(tri),
        ],
        out_specs=[
            pl.BlockSpec((rows, D_MODEL), lambda i: (i, 0)),
            pl.BlockSpec((rows, PACKED), lambda i: (i, 0)),
            pl.BlockSpec((TOP_K, rows), lambda i: (0, i)),
            pl.BlockSpec((TOP_K, rows), lambda i: (0, i)),
            pl.BlockSpec((TOP_K, rows), lambda i: (0, i)),
            pl.BlockSpec((N_EXPERTS, 1), const2),
        ],
        out_shape=[
            jax.ShapeDtypeStruct((t, D_MODEL), F32),
            jax.ShapeDtypeStruct((t, PACKED), U32),
            jax.ShapeDtypeStruct((TOP_K, t), I32),
            jax.ShapeDtypeStruct((TOP_K, t), F32),
            jax.ShapeDtypeStruct((TOP_K, t), I32),
            jax.ShapeDtypeStruct((N_EXPERTS, 1), F32),
        ],
        scratch_shapes=[pltpu.VMEM((N_EXPERTS, 1), F32)],
        compiler_params=pltpu.CompilerParams(
            dimension_semantics=("arbitrary",), vmem_limit_bytes=VMEM_LIMIT_BYTES),
        name="post_mix",
    )(mixin, x2, wo, g1, b1, wsg, wsu, wsd, wr_t, rbias, tri)


def _dispatch_kernel(pstart_ref, cnt_ref, nact_ref, eidx_ref, rank_ref, x_ref, xs_ref, zbuf, sem, zsem):
    i = pl.program_id(0)
    rows = x_ref.shape[0]
    n_blocks = xs_ref.shape[0] // EXPERT_ROWS

    def zero_copy(dst_row, n):
        return pltpu.make_async_copy(zbuf.at[pl.ds(0, n)], xs_ref.at[pl.ds(dst_row, n)], zsem)

    @pl.when(i == 0)
    def _():
        zbuf[...] = jnp.zeros_like(zbuf)

        def pad_walk(e, do):
            cnt = cnt_ref[e]
            npad = (EXPERT_ROWS - (cnt & (EXPERT_ROWS - 1))) & (EXPERT_ROWS - 1)
            row = pstart_ref[e] + cnt

            def one(r, c):
                do(zero_copy(row + r, 1))
                return c

            lax.fori_loop(0, npad, one, 0)

        def tail_walk(do):
            def body(b, c):
                do(zero_copy(pl.multiple_of(b * EXPERT_ROWS, EXPERT_ROWS), EXPERT_ROWS))
                return c
            lax.fori_loop(nact_ref[0], n_blocks, body, 0)

        def start_all(e, c):
            pad_walk(e, lambda cp: cp.start())
            return c

        def wait_all(e, c):
            pad_walk(e, lambda cp: cp.wait())
            return c

        lax.fori_loop(0, N_EXPERTS, start_all, 0)
        tail_walk(lambda cp: cp.start())
        lax.fori_loop(0, N_EXPERTS, wait_all, 0)
        tail_walk(lambda cp: cp.wait())

    def row_copy(t, dst):
        return pltpu.make_async_copy(x_ref.at[pl.ds(t, 1)], xs_ref.at[pl.ds(dst, 1)], sem)

    def issue(t, c):
        for k in range(TOP_K):
            dst = pstart_ref[eidx_ref[k, t]] + rank_ref[k, t]
            row_copy(t, dst).start()
        return c

    lax.fori_loop(0, rows, issue, 0)
    for k in range(TOP_K):
        pltpu.make_async_copy(x_ref, xs_ref.at[pl.ds(0, rows)], sem).wait()


def _dispatch(pad_start, counts, n_active, eidx, rank, xpk, n_rows):
    t = xpk.shape[0]
    rows = DISPATCH_ROWS
    smem_tile = pl.BlockSpec((TOP_K, rows), lambda i, *_: (0, i), memory_space=pltpu.SMEM)
    grid_spec = pltpu.PrefetchScalarGridSpec(
        num_scalar_prefetch=3,
        grid=(t // rows,),
        in_specs=[smem_tile, smem_tile, pl.BlockSpec((rows, PACKED), lambda i, *_: (i, 0))],
        out_specs=pl.BlockSpec(memory_space=pl.ANY),
        scratch_shapes=[
            pltpu.VMEM((EXPERT_ROWS, PACKED), U32),
            pltpu.SemaphoreType.DMA(()),
            pltpu.SemaphoreType.DMA(()),
        ],
    )
    return pl.pallas_call(
        _dispatch_kernel,
        grid_spec=grid_spec,
        out_shape=jax.ShapeDtypeStruct((n_rows, PACKED), U32),
        compiler_params=pltpu.CompilerParams(
            dimension_semantics=("arbitrary",), vmem_limit_bytes=VMEM_LIMIT_BYTES),
        name="dispatch",
    )(pad_start, counts, n_active, eidx, rank, xpk)


def _experts_kernel(bexp_ref, nact_ref, x_ref, wg_ref, wu_ref, wd_ref, y_ref, wg_b, wu_b, wd_b):
    i = pl.program_id(0)
    active = i < nact_ref[0]

    @pl.when(active)
    def _():
        prev = bexp_ref[jnp.maximum(i - 1, 0)]
        changed = jnp.logical_or(i == 0, bexp_ref[i] != prev)

        @pl.when(changed)
        def _():
            wg_b[...] = wg_ref[0].astype(BF16)
            wu_b[...] = wu_ref[0].astype(BF16)
            wd_b[...] = wd_ref[0].astype(BF16)

        w = x_ref[...]
        lo = lax.bitcast_convert_type(w << 16, F32).astype(BF16)
        hi = lax.bitcast_convert_type(w & jnp.uint32(0xFFFF0000), F32).astype(BF16)
        gate = _dot(lo, wg_b[:PACKED, :]) + _dot(hi, wg_b[PACKED:, :])
        up = _dot(lo, wu_b[:PACKED, :]) + _dot(hi, wu_b[PACKED:, :])
        hid = jax.nn.silu(gate) * up
        y_ref[...] = _dot(hid.astype(BF16), wd_b[...])

    @pl.when(jnp.logical_not(active))
    def _():
        y_ref[...] = jnp.zeros_like(y_ref)


def _experts(block_expert, n_active, xs, w_gate, w_up, w_down):
    n_rows = xs.shape[0]
    rows = EXPERT_ROWS
    n_blocks = n_rows // rows

    def live(i, nact):
        return jnp.minimum(i, nact[0] - 1)

    grid_spec = pltpu.PrefetchScalarGridSpec(
        num_scalar_prefetch=2,
        grid=(n_blocks,),
        in_specs=[
            pl.BlockSpec((rows, PACKED), lambda i, be, na: (live(i, na), 0)),
            pl.BlockSpec((1, D_MODEL, EXPERT_DIM), lambda i, be, na: (be[live(i, na)], 0, 0)),
            pl.BlockSpec((1, D_MODEL, EXPERT_DIM), lambda i, be, na: (be[live(i, na)], 0, 0)),
            pl.BlockSpec((1, EXPERT_DIM, D_MODEL), lambda i, be, na: (be[live(i, na)], 0, 0)),
        ],
        out_specs=pl.BlockSpec((rows, D_MODEL), lambda i, be, na: (i, 0)),
        scratch_shapes=[
            pltpu.VMEM((D_MODEL, EXPERT_DIM), BF16),
            pltpu.VMEM((D_MODEL, EXPERT_DIM), BF16),
            pltpu.VMEM((EXPERT_DIM, D_MODEL), BF16),
        ],
    )
    return pl.pallas_call(
        _experts_kernel,
        grid_spec=grid_spec,
        out_shape=jax.ShapeDtypeStruct((n_rows, D_MODEL), F32),
        compiler_params=pltpu.CompilerParams(
            dimension_semantics=("arbitrary",), vmem_limit_bytes=VMEM_LIMIT_BYTES),
        name="experts",
    )(block_expert, n_active, xs, w_gate, w_up, w_down)


def _combine_kernel(pstart_ref, eidx_ref, rank_ref, base_ref, gate_ref, g2_ref, b2_ref, ys_ref, o_ref, buf, sem):
    rows = base_ref.shape[0]

    def row_copy(k, t, src):
        return pltpu.make_async_copy(ys_ref.at[pl.ds(src, 1)], buf.at[k, pl.ds(t, 1)], sem)

    def issue(t, c):
        for k in range(TOP_K):
            src = pstart_ref[eidx_ref[k, t]] + rank_ref[k, t]
            row_copy(k, t, src).start()
        return c

    lax.fori_loop(0, rows, issue, 0)
    for k in range(TOP_K):
        pltpu.make_async_copy(ys_ref.at[pl.ds(0, rows)], buf.at[k], sem).wait()

    acc = base_ref[...]
    gates = gate_ref[...]
    for k in range(TOP_K):
        acc = acc + buf[k] * gates[:, k:k + 1]
    o_ref[...] = _layer_norm(acc, g2_ref[...], b2_ref[...])


def _combine(pad_start, eidx, rank, base, gates_t, g2, b2, ys):
    t = base.shape[0]
    rows = COMBINE_ROWS
    smem_tile = pl.BlockSpec((TOP_K, rows), lambda i, *_: (0, i), memory_space=pltpu.SMEM)
    grid_spec = pltpu.PrefetchScalarGridSpec(
        num_scalar_prefetch=1,
        grid=(t // rows,),
        in_specs=[
            smem_tile, smem_tile,
            pl.BlockSpec((rows, D_MODEL), lambda i, *_: (i, 0)),
            pl.BlockSpec((rows, TOP_K), lambda i, *_: (i, 0)),
            pl.BlockSpec((1, D_MODEL), lambda i, *_: (0, 0)),
            pl.BlockSpec((1, D_MODEL), lambda i, *_: (0, 0)),
            pl.BlockSpec(memory_space=pl.ANY),
        ],
        out_specs=pl.BlockSpec((rows, D_MODEL), lambda i, *_: (i, 0)),
        scratch_shapes=[
            pltpu.VMEM((TOP_K, rows, D_MODEL), F32),
            pltpu.SemaphoreType.DMA(()),
        ],
    )
    return pl.pallas_call(
        _combine_kernel,
        grid_spec=grid_spec,
        out_shape=jax.ShapeDtypeStruct((t, D_MODEL), F32),
        compiler_params=pltpu.CompilerParams(
            dimension_semantics=("arbitrary",), vmem_limit_bytes=VMEM_LIMIT_BYTES),
        name="combine",
    )(pad_start, eidx, rank, base, gates_t, g2, b2, ys)


def _rope_tables(seq):
    half = ROT_DIM // 2
    inv_freq = ROPE_THETA ** (-jnp.arange(half, dtype=F32) / half)
    ang = jnp.arange(seq).astype(F32)[:, None] * inv_freq[None, :]
    cos, sin = jnp.cos(ang), jnp.sin(ang)
    rest = HEAD_DIM - ROT_DIM
    ones = jnp.ones((seq, rest), F32)
    zeros = jnp.zeros((seq, rest), F32)
    zh = jnp.zeros((seq, half), F32)
    cos_h = jnp.concatenate([cos, cos, ones], axis=1)
    sneg_h = jnp.concatenate([-sin, zh, zeros], axis=1)
    spos_h = jnp.concatenate([zh, sin, zeros], axis=1)
    rep = LANES // HEAD_DIM
    return jnp.tile(cos_h, (1, rep)), jnp.tile(sneg_h, (1, rep)), jnp.tile(spos_h, (1, rep))


def _layer(alpha, x, w_in, w_pool, pool_scale, attn_sinks, w_o, ln1_g, ln1_b, w_router, router_bias,
           w_exp_gate, w_exp_up, w_exp_down, w_sh_gate, w_sh_up, w_sh_down, ln2_g, ln2_b):
    batch, seq, _ = x.shape
    t = batch * seq
    x2 = x.reshape(t, D_MODEL)

    c2 = POOL_WIDTH + ATTN_WIDTH
    kv_cols = []
    for base_col in (c2, c2 + KV_WIDTH):
        for h in range(N_KV_HEADS):
            w_h = w_in[:, base_col + h * HEAD_DIM:base_col + (h + 1) * HEAD_DIM]
            kv_cols += [w_h, w_h]
    w_ext = jnp.concatenate([w_in[:, :c2]] + kv_cols, axis=1).astype(BF16)
    cos_t, sneg_t, spos_t = _rope_tables(seq)

    u, q, kd, vd = _in_proj(x2, w_ext, cos_t, sneg_t, spos_t, seq)
    mixin = _mixer(u, q, kd, vd, w_pool.astype(BF16), pool_scale.reshape(1, POOL_WIDTH),
                   attn_sinks.astype(F32), batch, seq)

    r = lax.broadcasted_iota(I32, (POST_ROWS, POST_ROWS), 0)
    c = lax.broadcasted_iota(I32, (POST_ROWS, POST_ROWS), 1)
    tri = (r < c).astype(BF16)
    base, xpk, eidx, gates, rank, counts = _post_mix(
        alpha, mixin, x2, w_o.astype(BF16), ln1_g.reshape(1, D_MODEL), ln1_b.reshape(1, D_MODEL),
        w_sh_gate.astype(BF16), w_sh_up.astype(BF16), w_sh_down.astype(BF16),
        w_router.T.astype(BF16), router_bias.reshape(N_EXPERTS, 1).astype(F32), tri)

    n_blocks = (t * TOP_K + N_EXPERTS * (EXPERT_ROWS - 1)) // EXPERT_ROWS
    counts_i = counts[:, 0].astype(I32)
    padded = (counts_i + EXPERT_ROWS - 1) // EXPERT_ROWS * EXPERT_ROWS
    pad_end = jnp.cumsum(padded)
    pad_start = (pad_end - padded).astype(I32)
    n_active = (pad_end[-1:] // EXPERT_ROWS).astype(I32)
    block_expert = jnp.minimum(
        jnp.searchsorted(pad_end, jnp.arange(n_blocks, dtype=I32) * EXPERT_ROWS, side="right"),
        N_EXPERTS - 1).astype(I32)

    xs = _dispatch(pad_start, counts_i, n_active, eidx, rank, xpk, n_blocks * EXPERT_ROWS)
    ys = _experts(block_expert, n_active, xs, w_exp_gate, w_exp_up, w_exp_down)
    out = _combine(pad_start, eidx, rank, base, gates.T, ln2_g.reshape(1, D_MODEL),
                   ln2_b.reshape(1, D_MODEL), ys)
    return out.reshape(batch, seq, D_MODEL)


def kernel(x, w_in, w_pool, pool_scale, attn_sinks, w_o, ln1_g, ln1_b, w_router, router_bias,
           w_exp_gate, w_exp_up, w_exp_down, w_sh_gate, w_sh_up, w_sh_down, ln2_g, ln2_b):
    depth = w_in.shape[0]
    alpha = (2.0 * depth) ** 0.25
    for l in range(depth):
        x = _layer(alpha, x, w_in[l], w_pool[l], pool_scale[l], attn_sinks[l], w_o[l], ln1_g[l], ln1_b[l],
                   w_router[l], router_bias[l], w_exp_gate[l], w_exp_up[l], w_exp_down[l],
                   w_sh_gate[l], w_sh_up[l], w_sh_down[l], ln2_g[l], ln2_b[l])
    return x
```

```python
import functools

import jax
import jax.numpy as jnp
from jax import lax
from jax.experimental import pallas as pl
from jax.experimental.pallas import tpu as pltpu

F32 = jnp.float32
BF16 = jnp.bfloat16
I32 = jnp.int32
U32 = jnp.uint32

D_MODEL = 2048
POOL_WIDTH = 1024
POOL_WINDOWS = (2, 4, 8, 16)
POOL_GROUP_WIDTH = 256
ATTN_WIDTH = 1024
HEAD_DIM = 64
N_Q_HEADS = 16
N_KV_HEADS = 2
Q_PER_KV = 8
KV_WIDTH = 128
ATTN_BLOCK = 128
ROT_DIM = 16
ROPE_THETA = 500000.0
N_EXPERTS = 256
TOP_K = 8
N_EXPERT_GROUPS = 8
GROUP_SIZE = N_EXPERTS // N_EXPERT_GROUPS
TOPK_GROUPS = 4
EXPERT_DIM = 512
ROUTED_SCALE = 2.5
LN_EPS = 1e-5

LANES = 128
SUBLANES = 8
VMEM_LIMIT_BYTES = 56 * 1024 * 1024

PROJ_ROWS = 512
MIX_ROWS = 512
POST_ROWS = 256
DISPATCH_ROWS = 256
EXPERT_ROWS = 128
COMBINE_ROWS = 128
HALO = 16
PACKED = D_MODEL // 2

NEG_INF = float("-inf")


def _dot(a, b):
    return jnp.dot(a, b, preferred_element_type=F32)


def _dot_nt(a, b):
    return lax.dot_general(a, b, (((1,), (1,)), ((), ())), preferred_element_type=F32)


def _layer_norm(y, g, b):
    mu = jnp.mean(y, axis=-1, keepdims=True)
    yc = y - mu
    var = jnp.mean(yc * yc, axis=-1, keepdims=True)
    return yc * lax.rsqrt(var + LN_EPS) * g + b


def _in_proj_kernel(x_ref, w_ref, cos_ref, sneg_ref, spos_ref, u_ref, q_ref, k_ref, v_ref):
    xb = x_ref[...].astype(BF16)
    c = cos_ref[...]
    sn = sneg_ref[...]
    sp = spos_ref[...]

    def rope(t):
        return t * c + pltpu.roll(t, LANES - ROT_DIM // 2, 1) * sn + pltpu.roll(t, ROT_DIM // 2, 1) * sp

    u_ref[...] = _dot(xb, w_ref[:, :POOL_WIDTH])
    qf = _dot(xb, w_ref[:, POOL_WIDTH:POOL_WIDTH + ATTN_WIDTH])
    for j in range(ATTN_WIDTH // LANES):
        q_ref[:, j * LANES:(j + 1) * LANES] = rope(qf[:, j * LANES:(j + 1) * LANES]).astype(BF16)
    kv = _dot(xb, w_ref[:, POOL_WIDTH + ATTN_WIDTH:])
    for j in range(N_KV_HEADS):
        k_ref[:, j * LANES:(j + 1) * LANES] = rope(kv[:, j * LANES:(j + 1) * LANES]).astype(BF16)
    v_ref[...] = kv[:, N_KV_HEADS * LANES:].astype(BF16)


def _in_proj(x2, w_ext, cos_t, sneg_t, spos_t, seq):
    t = x2.shape[0]
    rows = PROJ_ROWS
    n_pos = seq // rows
    wcols = w_ext.shape[1]
    kvw = N_KV_HEADS * LANES
    pos_map = lambda i: (i % n_pos, 0)
    return pl.pallas_call(
        _in_proj_kernel,
        grid=(t // rows,),
        in_specs=[
            pl.BlockSpec((rows, D_MODEL), lambda i: (i, 0)),
            pl.BlockSpec((D_MODEL, wcols), lambda i: (0, 0)),
            pl.BlockSpec((rows, LANES), pos_map),
            pl.BlockSpec((rows, LANES), pos_map),
            pl.BlockSpec((rows, LANES), pos_map),
        ],
        out_specs=[
            pl.BlockSpec((rows, POOL_WIDTH), lambda i: (i, 0)),
            pl.BlockSpec((rows, ATTN_WIDTH), lambda i: (i, 0)),
            pl.BlockSpec((rows, kvw), lambda i: (i, 0)),
            pl.BlockSpec((rows, kvw), lambda i: (i, 0)),
        ],
        out_shape=[
            jax.ShapeDtypeStruct((t, POOL_WIDTH), F32),
            jax.ShapeDtypeStruct((t, ATTN_WIDTH), BF16),
            jax.ShapeDtypeStruct((t, kvw), BF16),
            jax.ShapeDtypeStruct((t, kvw), BF16),
        ],
        compiler_params=pltpu.CompilerParams(
            dimension_semantics=("arbitrary",), vmem_limit_bytes=VMEM_LIMIT_BYTES),
        name="in_proj",
    )(x2, w_ext, cos_t, sneg_t, spos_t)


def _mixer_kernel(sinks_ref, u_ref, uh_ref, q_ref, k_ref, kp_ref, v_ref, vp_ref, wp_ref, ps_ref,
                  o_ref, ubuf, kbuf, vbuf):
    j = pl.program_id(1)
    rows = u_ref.shape[0]
    first = j == 0

    ubuf[HALO:HALO + rows, :] = u_ref[...]

    @pl.when(first)
    def _():
        ubuf[0:HALO, :] = jnp.zeros((HALO, POOL_WIDTH), F32)

    @pl.when(jnp.logical_not(first))
    def _():
        ubuf[0:HALO, :] = uh_ref[...]

    pos = j * rows + lax.broadcasted_iota(I32, (rows, POOL_GROUP_WIDTH), 0)
    for g, w in enumerate(POOL_WINDOWS):
        cs = slice(g * POOL_GROUP_WIDTH, (g + 1) * POOL_GROUP_WIDTH)
        cur = ubuf[HALO:HALO + rows, cs]
        acc = cur
        for d in range(1, w):
            acc = acc + ubuf[HALO - d:HALO - d + rows, cs]
        cnt = jnp.minimum(pos + 1, w).astype(F32)
        pooled = acc / cnt - cur
        mixed = _dot(pooled.astype(BF16), wp_ref[g]) * ps_ref[:, cs]
        o_ref[:, cs] = mixed.astype(BF16)

    kbuf[ATTN_BLOCK:, :] = k_ref[...]
    vbuf[ATTN_BLOCK:, :] = v_ref[...]

    @pl.when(first)
    def _():
        kbuf[0:ATTN_BLOCK, :] = jnp.zeros((ATTN_BLOCK, kbuf.shape[1]), BF16)
        vbuf[0:ATTN_BLOCK, :] = jnp.zeros((ATTN_BLOCK, vbuf.shape[1]), BF16)

    @pl.when(jnp.logical_not(first))
    def _():
        kbuf[0:ATTN_BLOCK, :] = kp_ref[...]
        vbuf[0:ATTN_BLOCK, :] = vp_ref[...]

    qrow = lax.broadcasted_iota(I32, (ATTN_BLOCK, 2 * ATTN_BLOCK), 0)
    kcol = lax.broadcasted_iota(I32, (ATTN_BLOCK, 2 * ATTN_BLOCK), 1)
    in_window = jnp.logical_and(kcol > qrow, kcol <= qrow + ATTN_BLOCK)
    lane = lax.broadcasted_iota(I32, (ATTN_BLOCK, LANES), 1)
    low_half = lane < HEAD_DIM
    scale = HEAD_DIM ** -0.5

    def block_body(blk, carry):
        r0 = pl.multiple_of(blk * ATTN_BLOCK, ATTN_BLOCK)
        has_prev = jnp.logical_or(blk > 0, jnp.logical_not(first))
        first_key = jnp.where(has_prev, 0, ATTN_BLOCK)
        allowed = jnp.logical_and(in_window, kcol >= first_key)
        for h in range(N_KV_HEADS):
            k2 = kbuf[pl.ds(r0, 2 * ATTN_BLOCK), h * LANES:(h + 1) * LANES]
            v2 = vbuf[pl.ds(r0, 2 * ATTN_BLOCK), h * LANES:(h + 1) * LANES]
            parts = []
            for g in range(Q_PER_KV):
                head = h * Q_PER_KV + g
                tile = q_ref[pl.ds(r0, ATTN_BLOCK), (head // 2) * LANES:(head // 2 + 1) * LANES]
                keep = low_half if head % 2 == 0 else jnp.logical_not(low_half)
                parts.append(jnp.where(keep, tile, jnp.zeros_like(tile)))
            q8 = jnp.concatenate(parts, axis=0)
            s8 = _dot_nt(q8, k2) * scale
            probs = []
            for g in range(Q_PER_KV):
                sink = sinks_ref[h * Q_PER_KV + g]
                sg = jnp.where(allowed, s8[g * ATTN_BLOCK:(g + 1) * ATTN_BLOCK], NEG_INF)
                m = jnp.maximum(jnp.max(sg, axis=-1, keepdims=True), sink)
                p = jnp.exp(sg - m)
                denom = jnp.sum(p, axis=-1, keepdims=True) + jnp.exp(sink - m)
                probs.append((p / denom).astype(BF16))
            p8 = jnp.concatenate(probs, axis=0)
            o8 = _dot(p8, v2)
            for pair in range(Q_PER_KV // 2):
                lo = o8[(2 * pair) * ATTN_BLOCK:(2 * pair + 1) * ATTN_BLOCK]
                hi = o8[(2 * pair + 1) * ATTN_BLOCK:(2 * pair + 2) * ATTN_BLOCK]
                col = POOL_WIDTH + (h * (Q_PER_KV // 2) + pair) * LANES
                o_ref[pl.ds(r0, ATTN_BLOCK), col:col + LANES] = jnp.where(low_half, lo, hi).astype(BF16)
        return carry

    lax.fori_loop(0, rows // ATTN_BLOCK, block_body, 0)


def _mixer(u, q, kd, vd, wp, pool_scale, sinks, batch, seq):
    t = u.shape[0]
    rows = MIX_ROWS
    nq = seq // rows
    kvw = kd.shape[1]
    cur = lambda b, j: (b * nq + j, 0)
    halo = lambda b, j: (jnp.maximum((b * seq + j * rows) // HALO - 1, 0), 0)
    prev = lambda b, j: (jnp.maximum((b * seq + j * rows) // ATTN_BLOCK - 1, 0), 0)
    grid_spec = pltpu.PrefetchScalarGridSpec(
        num_scalar_prefetch=0,
        grid=(batch, nq),
        in_specs=[
            pl.BlockSpec(memory_space=pltpu.SMEM),
            pl.BlockSpec((rows, POOL_WIDTH), cur),
            pl.BlockSpec((HALO, POOL_WIDTH), halo),
            pl.BlockSpec((rows, ATTN_WIDTH), cur),
            pl.BlockSpec((rows, kvw), cur),
            pl.BlockSpec((ATTN_BLOCK, kvw), prev),
            pl.BlockSpec((rows, kvw), cur),
            pl.BlockSpec((ATTN_BLOCK, kvw), prev),
            pl.BlockSpec((len(POOL_WINDOWS), POOL_GROUP_WIDTH, POOL_GROUP_WIDTH), lambda b, j: (0, 0, 0)),
            pl.BlockSpec((1, POOL_WIDTH), lambda b, j: (0, 0)),
        ],
        out_specs=pl.BlockSpec((rows, D_MODEL), cur),
        scratch_shapes=[
            pltpu.VMEM((HALO + rows, POOL_WIDTH), F32),
            pltpu.VMEM((ATTN_BLOCK + rows, kvw), BF16),
            pltpu.VMEM((ATTN_BLOCK + rows, kvw), BF16),
        ],
    )
    return pl.pallas_call(
        _mixer_kernel,
        grid_spec=grid_spec,
        out_shape=jax.ShapeDtypeStruct((t, D_MODEL), BF16),
        compiler_params=pltpu.CompilerParams(
            dimension_semantics=("arbitrary", "arbitrary"), vmem_limit_bytes=VMEM_LIMIT_BYTES),
        name="mixer",
    )(sinks, u, u, q, kd, kd, vd, vd, wp, pool_scale)


def _post_mix_kernel(alpha, m_ref, x_ref, wo_ref, g1_ref, b1_ref, wsg_ref, wsu_ref, wsd_ref, wr_ref, rb_ref, tri_ref,
                     base_ref, xpk_ref, eidx_ref, gate_ref, rank_ref, cnt_ref, carry):
    i = pl.program_id(0)
    rows = m_ref.shape[0]

    @pl.when(i == 0)
    def _():
        carry[...] = jnp.zeros_like(carry)

    y = alpha * x_ref[...] + _dot(m_ref[...], wo_ref[...])
    x1 = _layer_norm(y, g1_ref[...], b1_ref[...])
    x1b = x1.astype(BF16)

    hid = jax.nn.silu(_dot(x1b, wsg_ref[...])) * _dot(x1b, wsu_ref[...])
    base_ref[...] = alpha * x1 + _dot(hid.astype(BF16), wsd_ref[...])

    bits = lax.bitcast_convert_type(x1b.astype(F32), U32)
    xpk_ref[...] = jnp.bitwise_or(bits[:, PACKED:] & jnp.uint32(0xFFFF0000), bits[:, :PACKED] >> 16)

    scores = jax.nn.sigmoid(_dot_nt(wr_ref[...], x1b))
    biased = scores + rb_ref[...]
    g3 = biased.reshape(N_EXPERT_GROUPS, GROUP_SIZE, rows)
    in_grp = lax.broadcasted_iota(I32, g3.shape, 1).astype(F32)
    top1 = jnp.max(g3, axis=1, keepdims=True)
    first1 = jnp.min(jnp.where(g3 == top1, in_grp, float(GROUP_SIZE)), axis=1, keepdims=True)
    top2 = jnp.max(jnp.where(in_grp == first1, NEG_INF, g3), axis=1, keepdims=True)
    gscore = top1 + top2
    gid = lax.broadcasted_iota(I32, gscore.shape, 0).astype(F32)
    gkeep = jnp.zeros(gscore.shape, F32)
    for _ in range(TOPK_GROUPS):
        m = jnp.max(gscore, axis=0, keepdims=True)
        pick = jnp.min(jnp.where(gscore == m, gid, float(N_EXPERT_GROUPS)), axis=0, keepdims=True)
        sel = gid == pick
        gkeep = jnp.where(sel, 1.0, gkeep)
        gscore = jnp.where(sel, NEG_INF, gscore)
    cand = jnp.where(gkeep > 0.0, g3, NEG_INF).reshape(N_EXPERTS, rows)

    eid = lax.broadcasted_iota(I32, (N_EXPERTS, rows), 0).astype(F32)
    chosen = jnp.zeros((N_EXPERTS, rows), F32)
    picks, pscores = [], []
    for _ in range(TOP_K):
        m = jnp.max(cand, axis=0, keepdims=True)
        pick = jnp.min(jnp.where(cand == m, eid, float(N_EXPERTS)), axis=0, keepdims=True)
        sel = eid == pick
        pscores.append(jnp.sum(jnp.where(sel, scores, 0.0), axis=0, keepdims=True))
        picks.append(pick)
        chosen = jnp.where(sel, 1.0, chosen)
        cand = jnp.where(sel, NEG_INF, cand)
    total = pscores[0]
    for s in pscores[1:]:
        total = total + s

    before = _dot(chosen.astype(BF16), tri_ref[...]) + carry[...]
    ranks = [jnp.sum(jnp.where(eid == p, before, 0.0), axis=0, keepdims=True) for p in picks]
    carry[...] = carry[...] + jnp.sum(chosen, axis=1, keepdims=True)

    eidx_ref[...] = jnp.concatenate(picks, axis=0).astype(I32)
    gate_ref[...] = jnp.concatenate([s / total * ROUTED_SCALE for s in pscores], axis=0)
    rank_ref[...] = jnp.concatenate(ranks, axis=0).astype(I32)
    cnt_ref[...] = carry[...]


def _post_mix(alpha, mixin, x2, wo, g1, b1, wsg, wsu, wsd, wr_t, rbias, tri):
    t = x2.shape[0]
    rows = POST_ROWS
    const2 = lambda i: (0, 0)
    full = lambda a: pl.BlockSpec(a.shape, const2, pipeline_mode=pl.Buffered(1))
    return pl.pallas_call(
        functools.partial(_post_mix_kernel, alpha),
        grid=(t // rows,),
        in_specs=[
            pl.BlockSpec((rows, D_MODEL), lambda i: (i, 0)),
            pl.BlockSpec((rows, D_MODEL), lambda i: (i, 0)),
            full(wo), full(g1), full(b1), full(wsg), full(wsu), full(wsd), full(wr_t), full(rbias), full(tri),
        ],
        out_specs=[
            pl.BlockSpec((rows, D_MODEL), lambda i: (i, 0)),
            pl.BlockSpec((rows, PACKED), lambda i: (i, 0)),
            pl.BlockSpec((TOP_K, rows), lambda i: (0, i)),
            pl.BlockSpec((TOP_K, rows), lambda i: (0, i)),
            pl.BlockSpec((TOP_K, rows), lambda i: (0, i)),
            pl.BlockSpec((N_EXPERTS, 1), const2),
        ],
        out_shape=[
            jax.ShapeDtypeStruct((t, D_MODEL), F32),
            jax.ShapeDtypeStruct((t, PACKED), U32),
            jax.ShapeDtypeStruct((TOP_K, t), I32),
            jax.ShapeDtypeStruct((TOP_K, t), F32),
            jax.ShapeDtypeStruct((TOP_K, t), I32),
            jax.ShapeDtypeStruct((N_EXPERTS, 1), F32),
        ],
        scratch_shapes=[pltpu.VMEM((N_EXPERTS, 1), F32)],
        compiler_params=pltpu.CompilerParams(
            dimension_semantics=("arbitrary",), vmem_limit_bytes=VMEM_LIMIT_BYTES),
        name="post_mix",
    )(mixin, x2, wo, g1, b1, wsg, wsu, wsd, wr_t, rbias, tri)


def _dispatch_kernel(pstart_ref, cnt_ref, nact_ref, eidx_ref, rank_ref, x_ref, xs_ref, zbuf, sem, zsem):
    i = pl.program_id(0)
    rows = x_ref.shape[0]
    n_blocks = xs_ref.shape[0] // EXPERT_ROWS

    def zero_copy(dst_row, n):
        return pltpu.make_async_copy(zbuf.at[pl.ds(0, n)], xs_ref.at[pl.ds(dst_row, n)], zsem)

    @pl.when(i == 0)
    def _():
        zbuf[...] = jnp.zeros_like(zbuf)

        def pad_walk(e, do):
            cnt = cnt_ref[e]
            npad = (EXPERT_ROWS - (cnt & (EXPERT_ROWS - 1))) & (EXPERT_ROWS - 1)
            row = pstart_ref[e] + cnt

            def one(r, c):
                do(zero_copy(row + r, 1))
                return c

            lax.fori_loop(0, npad, one, 0)

        def tail_walk(do):
            def body(b, c):
                do(zero_copy(pl.multiple_of(b * EXPERT_ROWS, EXPERT_ROWS), EXPERT_ROWS))
                return c
            lax.fori_loop(nact_ref[0], n_blocks, body, 0)

        def start_all(e, c):
            pad_walk(e, lambda cp: cp.start())
            return c

        def wait_all(e, c):
            pad_walk(e, lambda cp: cp.wait())
            return c

        lax.fori_loop(0, N_EXPERTS, start_all, 0)
        tail_walk(lambda cp: cp.start())
        lax.fori_loop(0, N_EXPERTS, wait_all, 0)
        tail_walk(lambda cp: cp.wait())

    def row_copy(t, dst):
        return pltpu.make_async_copy(x_ref.at[pl.ds(t, 1)], xs_ref.at[pl.ds(dst, 1)], sem)

    def issue(t, c):
        for k in range(TOP_K):
            dst = pstart_ref[eidx_ref[k, t]] + rank_ref[k, t]
            row_copy(t, dst).start(priority=k % 2)
        return c

    lax.fori_loop(0, rows, issue, 0)
    for k in range(TOP_K):
        pltpu.make_async_copy(x_ref, xs_ref.at[pl.ds(0, rows)], sem).wait()


def _dispatch(pad_start, counts, n_active, eidx, rank, xpk, n_rows):
    t = xpk.shape[0]
    rows = DISPATCH_ROWS
    smem_tile = pl.BlockSpec((TOP_K, rows), lambda i, *_: (0, i), memory_space=pltpu.SMEM)
    grid_spec = pltpu.PrefetchScalarGridSpec(
        num_scalar_prefetch=3,
        grid=(t // rows,),
        in_specs=[smem_tile, smem_tile, pl.BlockSpec((rows, PACKED), lambda i, *_: (i, 0))],
        out_specs=pl.BlockSpec(memory_space=pl.ANY),
        scratch_shapes=[
            pltpu.VMEM((EXPERT_ROWS, PACKED), U32),
            pltpu.SemaphoreType.DMA(()),
            pltpu.SemaphoreType.DMA(()),
        ],
    )
    return pl.pallas_call(
        _dispatch_kernel,
        grid_spec=grid_spec,
        out_shape=jax.ShapeDtypeStruct((n_rows, PACKED), U32),
        compiler_params=pltpu.CompilerParams(
            dimension_semantics=("arbitrary",), vmem_limit_bytes=VMEM_LIMIT_BYTES),
        name="dispatch",
    )(pad_start, counts, n_active, eidx, rank, xpk)


def _experts_kernel(bexp_ref, nact_ref, x_ref, wg_ref, wu_ref, wd_ref, y_ref, wg_b, wu_b, wd_b):
    i = pl.program_id(0)
    active = i < nact_ref[0]

    @pl.when(active)
    def _():
        prev = bexp_ref[jnp.maximum(i - 1, 0)]
        changed = jnp.logical_or(i == 0, bexp_ref[i] != prev)

        @pl.when(changed)
        def _():
            wg_b[...] = wg_ref[0].astype(BF16)
            wu_b[...] = wu_ref[0].astype(BF16)
            wd_b[...] = wd_ref[0].astype(BF16)

        w = x_ref[...]
        lo = lax.bitcast_convert_type(w << 16, F32).astype(BF16)
        hi = lax.bitcast_convert_type(w & jnp.uint32(0xFFFF0000), F32).astype(BF16)
        gate = _dot(lo, wg_b[:PACKED, :]) + _dot(hi, wg_b[PACKED:, :])
        up = _dot(lo, wu_b[:PACKED, :]) + _dot(hi, wu_b[PACKED:, :])
        hid = jax.nn.silu(gate) * up
        y_ref[...] = _dot(hid.astype(BF16), wd_b[...])

    @pl.when(jnp.logical_not(active))
    def _():
        y_ref[...] = jnp.zeros_like(y_ref)


def _experts(block_expert, n_active, xs, w_gate, w_up, w_down):
    n_rows = xs.shape[0]
    rows = EXPERT_ROWS
    n_blocks = n_rows // rows

    def live(i, nact):
        return jnp.minimum(i, nact[0] - 1)

    grid_spec = pltpu.PrefetchScalarGridSpec(
        num_scalar_prefetch=2,
        grid=(n_blocks,),
        in_specs=[
            pl.BlockSpec((rows, PACKED), lambda i, be, na: (live(i, na), 0)),
            pl.BlockSpec((1, D_MODEL, EXPERT_DIM), lambda i, be, na: (be[live(i, na)], 0, 0)),
            pl.BlockSpec((1, D_MODEL, EXPERT_DIM), lambda i, be, na: (be[live(i, na)], 0, 0)),
            pl.BlockSpec((1, EXPERT_DIM, D_MODEL), lambda i, be, na: (be[live(i, na)], 0, 0)),
        ],
        out_specs=pl.BlockSpec((rows, D_MODEL), lambda i, be, na: (i, 0)),
        scratch_shapes=[
            pltpu.VMEM((D_MODEL, EXPERT_DIM), BF16),
            pltpu.VMEM((D_MODEL, EXPERT_DIM), BF16),
            pltpu.VMEM((EXPERT_DIM, D_MODEL), BF16),
        ],
    )
    return pl.pallas_call(
        _experts_kernel,
        grid_spec=grid_spec,
        out_shape=jax.ShapeDtypeStruct((n_rows, D_MODEL), F32),
        compiler_params=pltpu.CompilerParams(
            dimension_semantics=("arbitrary",), vmem_limit_bytes=VMEM_LIMIT_BYTES),
        name="experts",
    )(block_expert, n_active, xs, w_gate, w_up, w_down)


def _combine_kernel(pstart_ref, eidx_ref, rank_ref, base_ref, gate_ref, g2_ref, b2_ref, ys_ref, o_ref, buf, sem):
    rows = base_ref.shape[0]

    def row_copy(k, t, src):
        return pltpu.make_async_copy(ys_ref.at[pl.ds(src, 1)], buf.at[k, pl.ds(t, 1)], sem)

    def issue(t, c):
        for k in range(TOP_K):
            src = pstart_ref[eidx_ref[k, t]] + rank_ref[k, t]
            row_copy(k, t, src).start(priority=k % 2)
        return c

    lax.fori_loop(0, rows, issue, 0)
    for k in range(TOP_K):
        pltpu.make_async_copy(ys_ref.at[pl.ds(0, rows)], buf.at[k], sem).wait()

    acc = base_ref[...]
    gates = gate_ref[...]
    for k in range(TOP_K):
        acc = acc + buf[k] * gates[:, k:k + 1]
    o_ref[...] = _layer_norm(acc, g2_ref[...], b2_ref[...])


def _combine(pad_start, eidx, rank, base, gates_t, g2, b2, ys):
    t = base.shape[0]
    rows = COMBINE_ROWS
    smem_tile = pl.BlockSpec((TOP_K, rows), lambda i, *_: (0, i), memory_space=pltpu.SMEM)
    grid_spec = pltpu.PrefetchScalarGridSpec(
        num_scalar_prefetch=1,
        grid=(t // rows,),
        in_specs=[
            smem_tile, smem_tile,
            pl.BlockSpec((rows, D_MODEL), lambda i, *_: (i, 0)),
            pl.BlockSpec((rows, TOP_K), lambda i, *_: (i, 0)),
            pl.BlockSpec((1, D_MODEL), lambda i, *_: (0, 0)),
            pl.BlockSpec((1, D_MODEL), lambda i, *_: (0, 0)),
            pl.BlockSpec(memory_space=pl.ANY),
        ],
        out_specs=pl.BlockSpec((rows, D_MODEL), lambda i, *_: (i, 0)),
        scratch_shapes=[
            pltpu.VMEM((TOP_K, rows, D_MODEL), F32),
            pltpu.SemaphoreType.DMA(()),
        ],
    )
    return pl.pallas_call(
        _combine_kernel,
        grid_spec=grid_spec,
        out_shape=jax.ShapeDtypeStruct((t, D_MODEL), F32),
        compiler_params=pltpu.CompilerParams(
            dimension_semantics=("arbitrary",), vmem_limit_bytes=VMEM_LIMIT_BYTES),
        name="combine",
    )(pad_start, eidx, rank, base, gates_t, g2, b2, ys)


def _rope_tables(seq):
    half = ROT_DIM // 2
    inv_freq = ROPE_THETA ** (-jnp.arange(half, dtype=F32) / half)
    ang = jnp.arange(seq).astype(F32)[:, None] * inv_freq[None, :]
    cos, sin = jnp.cos(ang), jnp.sin(ang)
    rest = HEAD_DIM - ROT_DIM
    ones = jnp.ones((seq, rest), F32)
    zeros = jnp.zeros((seq, rest), F32)
    zh = jnp.zeros((seq, half), F32)
    cos_h = jnp.concatenate([cos, cos, ones], axis=1)
    sneg_h = jnp.concatenate([-sin, zh, zeros], axis=1)
    spos_h = jnp.concatenate([zh, sin, zeros], axis=1)
    rep = LANES // HEAD_DIM
    return jnp.tile(cos_h, (1, rep)), jnp.tile(sneg_h, (1, rep)), jnp.tile(spos_h, (1, rep))


def _layer(alpha, x, w_in, w_pool, pool_scale, attn_sinks, w_o, ln1_g, ln1_b, w_router, router_bias,
           w_exp_gate, w_exp_up, w_exp_down, w_sh_gate, w_sh_up, w_sh_down, ln2_g, ln2_b):
    batch, seq, _ = x.shape
    t = batch * seq
    x2 = x.reshape(t, D_MODEL)

    c2 = POOL_WIDTH + ATTN_WIDTH
    kv_cols = []
    for base_col in (c2, c2 + KV_WIDTH):
        for h in range(N_KV_HEADS):
            w_h = w_in[:, base_col + h * HEAD_DIM:base_col + (h + 1) * HEAD_DIM]
            kv_cols += [w_h, w_h]
    w_ext = jnp.concatenate([w_in[:, :c2]] + kv_cols, axis=1).astype(BF16)
    cos_t, sneg_t, spos_t = _rope_tables(seq)

    u, q, kd, vd = _in_proj(x2, w_ext, cos_t, sneg_t, spos_t, seq)
    mixin = _mixer(u, q, kd, vd, w_pool.astype(BF16), pool_scale.reshape(1, POOL_WIDTH),
                   attn_sinks.astype(F32), batch, seq)

    r = lax.broadcasted_iota(I32, (POST_ROWS, POST_ROWS), 0)
    c = lax.broadcasted_iota(I32, (POST_ROWS, POST_ROWS), 1)
    tri = (r < c).astype(BF16)
    base, xpk, eidx, gates, rank, counts = _post_mix(
        alpha, mixin, x2, w_o.astype(BF16), ln1_g.reshape(1, D_MODEL), ln1_b.reshape(1, D_MODEL),
        w_sh_gate.astype(BF16), w_sh_up.astype(BF16), w_sh_down.astype(BF16),
        w_router.T.astype(BF16), router_bias.reshape(N_EXPERTS, 1).astype(F32), tri)

    n_blocks = (t * TOP_K + N_EXPERTS * (EXPERT_ROWS - 1)) // EXPERT_ROWS
    counts_i = counts[:, 0].astype(I32)
    padded = (counts_i + EXPERT_ROWS - 1) // EXPERT_ROWS * EXPERT_ROWS
    pad_end = jnp.cumsum(padded)
    pad_start = (pad_end - padded).astype(I32)
    n_active = (pad_end[-1:] // EXPERT_ROWS).astype(I32)
    block_expert = jnp.minimum(
        jnp.searchsorted(pad_end, jnp.arange(n_blocks, dtype=I32) * EXPERT_ROWS, side="right"),
        N_EXPERTS - 1).astype(I32)

    xs = _dispatch(pad_start, counts_i, n_active, eidx, rank, xpk, n_blocks * EXPERT_ROWS)
    ys = _experts(block_expert, n_active, xs, w_exp_gate, w_exp_up, w_exp_down)
    out = _combine(pad_start, eidx, rank, base, gates.T, ln2_g.reshape(1, D_MODEL),
                   ln2_b.reshape(1, D_MODEL), ys)
    return out.reshape(batch, seq, D_MODEL)


def kernel(x, w_in, w_pool, pool_scale, attn_sinks, w_o, ln1_g, ln1_b, w_router, router_bias,
           w_exp_gate, w_exp_up, w_exp_down, w_sh_gate, w_sh_up, w_sh_down, ln2_g, ln2_b):
    depth = w_in.shape[0]
    alpha = (2.0 * depth) ** 0.25
    for l in range(depth):
        x = _layer(alpha, x, w_in[l], w_pool[l], pool_scale[l], attn_sinks[l], w_o[l], ln1_g[l], ln1_b[l],
                   w_router[l], router_bias[l], w_exp_gate[l], w_exp_up[l], w_exp_down[l],
                   w_sh_gate[l], w_sh_up[l], w_sh_down[l], ln2_g[l], ln2_b[l])
    return x
```

```python
import functools

import jax
import jax.numpy as jnp
from jax import lax
from jax.experimental import pallas as pl
from jax.experimental.pallas import tpu as pltpu

F32 = jnp.float32
BF16 = jnp.bfloat16
I32 = jnp.int32
U32 = jnp.uint32

D_MODEL = 2048
POOL_WIDTH = 1024
POOL_WINDOWS = (2, 4, 8, 16)
POOL_GROUP_WIDTH = 256
ATTN_WIDTH = 1024
HEAD_DIM = 64
N_Q_HEADS = 16
N_KV_HEADS = 2
Q_PER_KV = 8
KV_WIDTH = 128
ATTN_BLOCK = 128
ROT_DIM = 16
ROPE_THETA = 500000.0
N_EXPERTS = 256
TOP_K = 8
N_EXPERT_GROUPS = 8
GROUP_SIZE = N_EXPERTS // N_EXPERT_GROUPS
TOPK_GROUPS = 4
EXPERT_DIM = 512
ROUTED_SCALE = 2.5
LN_EPS = 1e-5

LANES = 128
SUBLANES = 8
VMEM_LIMIT_BYTES = 56 * 1024 * 1024

PROJ_ROWS = 512
MIX_ROWS = 512
POST_ROWS = 256
DISPATCH_ROWS = 256
EXPERT_ROWS = 128
COMBINE_ROWS = 128
HALO = 16
PACKED = D_MODEL // 2

NEG_INF = float("-inf")


def _dot(a, b):
    return jnp.dot(a, b, preferred_element_type=F32)


def _dot_nt(a, b):
    return lax.dot_general(a, b, (((1,), (1,)), ((), ())), preferred_element_type=F32)


def _layer_norm(y, g, b):
    mu = jnp.mean(y, axis=-1, keepdims=True)
    yc = y - mu
    var = jnp.mean(yc * yc, axis=-1, keepdims=True)
    return yc * lax.rsqrt(var + LN_EPS) * g + b


def _in_proj_kernel(x_ref, w_ref, cos_ref, sneg_ref, spos_ref, u_ref, q_ref, k_ref, v_ref):
    xb = x_ref[...].astype(BF16)
    c = cos_ref[...]
    sn = sneg_ref[...]
    sp = spos_ref[...]

    def rope(t):
        return t * c + pltpu.roll(t, LANES - ROT_DIM // 2, 1) * sn + pltpu.roll(t, ROT_DIM // 2, 1) * sp

    u_ref[...] = _dot(xb, w_ref[:, :POOL_WIDTH])
    qf = _dot(xb, w_ref[:, POOL_WIDTH:POOL_WIDTH + ATTN_WIDTH])
    for j in range(ATTN_WIDTH // LANES):
        q_ref[:, j * LANES:(j + 1) * LANES] = rope(qf[:, j * LANES:(j + 1) * LANES]).astype(BF16)
    kv = _dot(xb, w_ref[:, POOL_WIDTH + ATTN_WIDTH:])
    for j in range(N_KV_HEADS):
        k_ref[:, j * LANES:(j + 1) * LANES] = rope(kv[:, j * LANES:(j + 1) * LANES]).astype(BF16)
    v_ref[...] = kv[:, N_KV_HEADS * LANES:].astype(BF16)


def _in_proj(x2, w_ext, cos_t, sneg_t, spos_t, seq):
    t = x2.shape[0]
    rows = PROJ_ROWS
    n_pos = seq // rows
    wcols = w_ext.shape[1]
    kvw = N_KV_HEADS * LANES
    pos_map = lambda i: (i % n_pos, 0)
    return pl.pallas_call(
        _in_proj_kernel,
        grid=(t // rows,),
        in_specs=[
            pl.BlockSpec((rows, D_MODEL), lambda i: (i, 0)),
            pl.BlockSpec((D_MODEL, wcols), lambda i: (0, 0)),
            pl.BlockSpec((rows, LANES), pos_map),
            pl.BlockSpec((rows, LANES), pos_map),
            pl.BlockSpec((rows, LANES), pos_map),
        ],
        out_specs=[
            pl.BlockSpec((rows, POOL_WIDTH), lambda i: (i, 0)),
            pl.BlockSpec((rows, ATTN_WIDTH), lambda i: (i, 0)),
            pl.BlockSpec((rows, kvw), lambda i: (i, 0)),
            pl.BlockSpec((rows, kvw), lambda i: (i, 0)),
        ],
        out_shape=[
            jax.ShapeDtypeStruct((t, POOL_WIDTH), F32),
            jax.ShapeDtypeStruct((t, ATTN_WIDTH), BF16),
            jax.ShapeDtypeStruct((t, kvw), BF16),
            jax.ShapeDtypeStruct((t, kvw), BF16),
        ],
        compiler_params=pltpu.CompilerParams(
            dimension_semantics=("arbitrary",), vmem_limit_bytes=VMEM_LIMIT_BYTES),
        name="in_proj",
    )(x2, w_ext, cos_t, sneg_t, spos_t)


def _mixer_kernel(sinks_ref, u_ref, uh_ref, q_ref, k_ref, kp_ref, v_ref, vp_ref, wp_ref, ps_ref,
                  o_ref, ubuf, kbuf, vbuf):
    j = pl.program_id(1)
    rows = u_ref.shape[0]
    first = j == 0

    ubuf[HALO:HALO + rows, :] = u_ref[...]

    @pl.when(first)
    def _():
        ubuf[0:HALO, :] = jnp.zeros((HALO, POOL_WIDTH), F32)

    @pl.when(jnp.logical_not(first))
    def _():
        ubuf[0:HALO, :] = uh_ref[...]

    pos = j * rows + lax.broadcasted_iota(I32, (rows, POOL_GROUP_WIDTH), 0)
    for g, w in enumerate(POOL_WINDOWS):
        cs = slice(g * POOL_GROUP_WIDTH, (g + 1) * POOL_GROUP_WIDTH)
        cur = ubuf[HALO:HALO + rows, cs]
        acc = cur
        for d in range(1, w):
            acc = acc + ubuf[HALO - d:HALO - d + rows, cs]
        cnt = jnp.minimum(pos + 1, w).astype(F32)
        pooled = acc / cnt - cur
        mixed = _dot(pooled.astype(BF16), wp_ref[g]) * ps_ref[:, cs]
        o_ref[:, cs] = mixed.astype(BF16)

    kbuf[ATTN_BLOCK:, :] = k_ref[...]
    vbuf[ATTN_BLOCK:, :] = v_ref[...]

    @pl.when(first)
    def _():
        kbuf[0:ATTN_BLOCK, :] = jnp.zeros((ATTN_BLOCK, kbuf.shape[1]), BF16)
        vbuf[0:ATTN_BLOCK, :] = jnp.zeros((ATTN_BLOCK, vbuf.shape[1]), BF16)

    @pl.when(jnp.logical_not(first))
    def _():
        kbuf[0:ATTN_BLOCK, :] = kp_ref[...]
        vbuf[0:ATTN_BLOCK, :] = vp_ref[...]

    qrow = lax.broadcasted_iota(I32, (ATTN_BLOCK, 2 * ATTN_BLOCK), 0)
    kcol = lax.broadcasted_iota(I32, (ATTN_BLOCK, 2 * ATTN_BLOCK), 1)
    in_window = jnp.logical_and(kcol > qrow, kcol <= qrow + ATTN_BLOCK)
    lane = lax.broadcasted_iota(I32, (ATTN_BLOCK, LANES), 1)
    low_half = lane < HEAD_DIM
    scale = HEAD_DIM ** -0.5

    def block_body(blk, carry):
        r0 = pl.multiple_of(blk * ATTN_BLOCK, ATTN_BLOCK)
        has_prev = jnp.logical_or(blk > 0, jnp.logical_not(first))
        first_key = jnp.where(has_prev, 0, ATTN_BLOCK)
        allowed = jnp.logical_and(in_window, kcol >= first_key)
        for h in range(N_KV_HEADS):
            k2 = kbuf[pl.ds(r0, 2 * ATTN_BLOCK), h * LANES:(h + 1) * LANES]
            v2 = vbuf[pl.ds(r0, 2 * ATTN_BLOCK), h * LANES:(h + 1) * LANES]
            parts = []
            for g in range(Q_PER_KV):
                head = h * Q_PER_KV + g
                tile = q_ref[pl.ds(r0, ATTN_BLOCK), (head // 2) * LANES:(head // 2 + 1) * LANES]
                keep = low_half if head % 2 == 0 else jnp.logical_not(low_half)
                parts.append(jnp.where(keep, tile, jnp.zeros_like(tile)))
            q8 = jnp.concatenate(parts, axis=0)
            s8 = _dot_nt(q8, k2) * scale
            probs = []
            for g in range(Q_PER_KV):
                sink = sinks_ref[h * Q_PER_KV + g]
                sg = jnp.where(allowed, s8[g * ATTN_BLOCK:(g + 1) * ATTN_BLOCK], NEG_INF)
                m = jnp.maximum(jnp.max(sg, axis=-1, keepdims=True), sink)
                p = jnp.exp(sg - m)
                denom = jnp.sum(p, axis=-1, keepdims=True) + jnp.exp(sink - m)
                probs.append((p / denom).astype(BF16))
            p8 = jnp.concatenate(probs, axis=0)
            o8 = _dot(p8, v2)
            for pair in range(Q_PER_KV // 2):
                lo = o8[(2 * pair) * ATTN_BLOCK:(2 * pair + 1) * ATTN_BLOCK]
                hi = o8[(2 * pair + 1) * ATTN_BLOCK:(2 * pair + 2) * ATTN_BLOCK]
                col = POOL_WIDTH + (h * (Q_PER_KV // 2) + pair) * LANES
                o_ref[pl.ds(r0, ATTN_BLOCK), col:col + LANES] = jnp.where(low_half, lo, hi).astype(BF16)
        return carry

    lax.fori_loop(0, rows // ATTN_BLOCK, block_body, 0)


def _mixer(u, q, kd, vd, wp, pool_scale, sinks, batch, seq):
    t = u.shape[0]
    rows = MIX_ROWS
    nq = seq // rows
    kvw = kd.shape[1]
    cur = lambda b, j: (b * nq + j, 0)
    halo = lambda b, j: (jnp.maximum((b * seq + j * rows) // HALO - 1, 0), 0)
    prev = lambda b, j: (jnp.maximum((b * seq + j * rows) // ATTN_BLOCK - 1, 0), 0)
    grid_spec = pltpu.PrefetchScalarGridSpec(
        num_scalar_prefetch=0,
        grid=(batch, nq),
        in_specs=[
            pl.BlockSpec(memory_space=pltpu.SMEM),
            pl.BlockSpec((rows, POOL_WIDTH), cur),
            pl.BlockSpec((HALO, POOL_WIDTH), halo),
            pl.BlockSpec((rows, ATTN_WIDTH), cur),
            pl.BlockSpec((rows, kvw), cur),
            pl.BlockSpec((ATTN_BLOCK, kvw), prev),
            pl.BlockSpec((rows, kvw), cur),
            pl.BlockSpec((ATTN_BLOCK, kvw), prev),
            pl.BlockSpec((len(POOL_WINDOWS), POOL_GROUP_WIDTH, POOL_GROUP_WIDTH), lambda b, j: (0, 0, 0)),
            pl.BlockSpec((1, POOL_WIDTH), lambda b, j: (0, 0)),
        ],
        out_specs=pl.BlockSpec((rows, D_MODEL), cur),
        scratch_shapes=[
            pltpu.VMEM((HALO + rows, POOL_WIDTH), F32),
            pltpu.VMEM((ATTN_BLOCK + rows, kvw), BF16),
            pltpu.VMEM((ATTN_BLOCK + rows, kvw), BF16),
        ],
    )
    return pl.pallas_call(
        _mixer_kernel,
        grid_spec=grid_spec,
        out_shape=jax.ShapeDtypeStruct((t, D_MODEL), BF16),
        compiler_params=pltpu.CompilerParams(
            dimension_semantics=("arbitrary", "arbitrary"), vmem_limit_bytes=VMEM_LIMIT_BYTES),
        name="mixer",
    )(sinks, u, u, q, kd, kd, vd, vd, wp, pool_scale)


def _post_mix_kernel(alpha, m_ref, x_ref, wo_ref, g1_ref, b1_ref, wsg_ref, wsu_ref, wsd_ref, wr_ref, rb_ref, tri_ref,
                     base_ref, xpk_ref, eidx_ref, gate_ref, rank_ref, cnt_ref, carry):
    i = pl.program_id(0)
    rows = m_ref.shape[0]

    @pl.when(i == 0)
    def _():
        carry[...] = jnp.zeros_like(carry)

    y = alpha * x_ref[...] + _dot(m_ref[...], wo_ref[...])
    x1 = _layer_norm(y, g1_ref[...], b1_ref[...])
    x1b = x1.astype(BF16)

    hid = jax.nn.silu(_dot(x1b, wsg_ref[...])) * _dot(x1b, wsu_ref[...])
    base_ref[...] = alpha * x1 + _dot(hid.astype(BF16), wsd_ref[...])

    bits = lax.bitcast_convert_type(x1b.astype(F32), U32)
    xpk_ref[...] = jnp.bitwise_or(bits[:, PACKED:] & jnp.uint32(0xFFFF0000), bits[:, :PACKED] >> 16)

    scores = jax.nn.sigmoid(_dot_nt(wr_ref[...], x1b))
    biased = scores + rb_ref[...]
    g3 = biased.reshape(N_EXPERT_GROUPS, GROUP_SIZE, rows)
    in_grp = lax.broadcasted_iota(I32, g3.shape, 1).astype(F32)
    top1 = jnp.max(g3, axis=1, keepdims=True)
    first1 = jnp.min(jnp.where(g3 == top1, in_grp, float(GROUP_SIZE)), axis=1, keepdims=True)
    top2 = jnp.max(jnp.where(in_grp == first1, NEG_INF, g3), axis=1, keepdims=True)
    gscore = top1 + top2
    gid = lax.broadcasted_iota(I32, gscore.shape, 0).astype(F32)
    gkeep = jnp.zeros(gscore.shape, F32)
    for _ in range(TOPK_GROUPS):
        m = jnp.max(gscore, axis=0, keepdims=True)
        pick = jnp.min(jnp.where(gscore == m, gid, float(N_EXPERT_GROUPS)), axis=0, keepdims=True)
        sel = gid == pick
        gkeep = jnp.where(sel, 1.0, gkeep)
        gscore = jnp.where(sel, NEG_INF, gscore)
    cand = jnp.where(gkeep > 0.0, g3, NEG_INF).reshape(N_EXPERTS, rows)

    eid = lax.broadcasted_iota(I32, (N_EXPERTS, rows), 0).astype(F32)
    chosen = jnp.zeros((N_EXPERTS, rows), F32)
    picks, pscores = [], []
    for _ in range(TOP_K):
        m = jnp.max(cand, axis=0, keepdims=True)
        pick = jnp.min(jnp.where(cand == m, eid, float(N_EXPERTS)), axis=0, keepdims=True)
        sel = eid == pick
        pscores.append(jnp.sum(jnp.where(sel, scores, 0.0), axis=0, keepdims=True))
        picks.append(pick)
        chosen = jnp.where(sel, 1.0, chosen)
        cand = jnp.where(sel, NEG_INF, cand)
    total = pscores[0]
    for s in pscores[1:]:
        total = total + s

    before = _dot(chosen.astype(BF16), tri_ref[...]) + carry[...]
    ranks = [jnp.sum(jnp.where(eid == p, before, 0.0), axis=0, keepdims=True) for p in picks]
    carry[...] = carry[...] + jnp.sum(chosen, axis=1, keepdims=True)

    eidx_ref[...] = jnp.concatenate(picks, axis=0).astype(I32)
    gate_ref[...] = jnp.concatenate([s / total * ROUTED_SCALE for s in pscores], axis=0)
    rank_ref[...] = jnp.concatenate(ranks, axis=0).astype(I32)
    cnt_ref[...] = carry[...]


def _post_mix(alpha, mixin, x2, wo, g1, b1, wsg, wsu, wsd, wr_t, rbias, tri):
    t = x2.shape[0]
    rows = POST_ROWS
    const2 = lambda i: (0, 0)
    full = lambda a: pl.BlockSpec(a.shape, const2, pipeline_mode=pl.Buffered(1))
    return pl.pallas_call(
        functools.partial(_post_mix_kernel, alpha),
        grid=(t // rows,),
        in_specs=[
            pl.BlockSpec((rows, D_MODEL), lambda i: (i, 0)),
            pl.BlockSpec((rows, D_MODEL), lambda i: (i, 0)),
            full(wo), full(g1), full(b1), full(wsg), full(wsu), full(wsd), full(wr_t), full(rbias), full(tri),
        ],
        out_specs=[
            pl.BlockSpec((rows, D_MODEL), lambda i: (i, 0)),
            pl.BlockSpec((rows, PACKED), lambda i: (i, 0)),
            pl.BlockSpec((TOP_K, rows), lambda i: (0, i)),
            pl.BlockSpec((TOP_K, rows), lambda i: (0, i)),
            pl.BlockSpec((TOP_K, rows), lambda i: (0, i)),
            pl.BlockSpec((N_EXPERTS, 1), const2),
        ],
        out_shape=[
            jax.ShapeDtypeStruct((t, D_MODEL), F32),
            jax.ShapeDtypeStruct((t, PACKED), U32),
            jax.ShapeDtypeStruct((TOP_K, t), I32),
            jax.ShapeDtypeStruct((TOP_K, t), F32),
            jax.ShapeDtypeStruct((TOP_K, t), I32),
            jax.ShapeDtypeStruct((N_EXPERTS, 1), F32),
        ],
        scratch_shapes=[pltpu.VMEM((N_EXPERTS, 1), F32)],
        compiler_params=pltpu.CompilerParams(
            dimension_semantics=("arbitrary",), vmem_limit_bytes=VMEM_LIMIT_BYTES),
        name="post_mix",
    )(mixin, x2, wo, g1, b1, wsg, wsu, wsd, wr_t, rbias, tri)


def _dispatch_kernel(pstart_ref, cnt_ref, nact_ref, eidx_ref, rank_ref, x_ref, xs_ref, zbuf, sem, zsem):
    i = pl.program_id(0)
    rows = x_ref.shape[0]
    n_blocks = xs_ref.shape[0] // EXPERT_ROWS

    def zero_copy(dst_row, n):
        return pltpu.make_async_copy(zbuf.at[pl.ds(0, n)], xs_ref.at[pl.ds(dst_row, n)], zsem)

    @pl.when(i == 0)
    def _():
        zbuf[...] = jnp.zeros_like(zbuf)

        def pad_walk(e, do):
            cnt = cnt_ref[e]
            npad = (EXPERT_ROWS - (cnt & (EXPERT_ROWS - 1))) & (EXPERT_ROWS - 1)
            row = pstart_ref[e] + cnt

            def one(r, c):
                do(zero_copy(row + r, 1))
                return c

            lax.fori_loop(0, npad, one, 0)

        def tail_walk(do):
            def body(b, c):
                do(zero_copy(pl.multiple_of(b * EXPERT_ROWS, EXPERT_ROWS), EXPERT_ROWS))
                return c
            lax.fori_loop(nact_ref[0], n_blocks, body, 0)

        def start_all(e, c):
            pad_walk(e, lambda cp: cp.start())
            return c

        def wait_all(e, c):
            pad_walk(e, lambda cp: cp.wait())
            return c

        lax.fori_loop(0, N_EXPERTS, start_all, 0)
        tail_walk(lambda cp: cp.start())
        lax.fori_loop(0, N_EXPERTS, wait_all, 0)
        tail_walk(lambda cp: cp.wait())

    def row_copy(t, dst):
        return pltpu.make_async_copy(x_ref.at[pl.ds(t, 1)], xs_ref.at[pl.ds(dst, 1)], sem)

    def issue(t, c):
        for k in range(TOP_K):
            dst = pstart_ref[eidx_ref[k, t]] + rank_ref[k, t]
            row_copy(t, dst).start(priority=k % 2)
        return c

    lax.fori_loop(0, rows, issue, 0, unroll=4)
    for k in range(TOP_K):
        pltpu.make_async_copy(x_ref, xs_ref.at[pl.ds(0, rows)], sem).wait()


def _dispatch(pad_start, counts, n_active, eidx, rank, xpk, n_rows):
    t = xpk.shape[0]
    rows = DISPATCH_ROWS
    smem_tile = pl.BlockSpec((TOP_K, rows), lambda i, *_: (0, i), memory_space=pltpu.SMEM)
    grid_spec = pltpu.PrefetchScalarGridSpec(
        num_scalar_prefetch=3,
        grid=(t // rows,),
        in_specs=[smem_tile, smem_tile, pl.BlockSpec((rows, PACKED), lambda i, *_: (i, 0))],
        out_specs=pl.BlockSpec(memory_space=pl.ANY),
        scratch_shapes=[
            pltpu.VMEM((EXPERT_ROWS, PACKED), U32),
            pltpu.SemaphoreType.DMA(()),
            pltpu.SemaphoreType.DMA(()),
        ],
    )
    return pl.pallas_call(
        _dispatch_kernel,
        grid_spec=grid_spec,
        out_shape=jax.ShapeDtypeStruct((n_rows, PACKED), U32),
        compiler_params=pltpu.CompilerParams(
            dimension_semantics=("arbitrary",), vmem_limit_bytes=VMEM_LIMIT_BYTES),
        name="dispatch",
    )(pad_start, counts, n_active, eidx, rank, xpk)


def _experts_kernel(bexp_ref, nact_ref, x_ref, wg_ref, wu_ref, wd_ref, y_ref, wg_b, wu_b, wd_b):
    i = pl.program_id(0)
    active = i < nact_ref[0]

    @pl.when(active)
    def _():
        prev = bexp_ref[jnp.maximum(i - 1, 0)]
        changed = jnp.logical_or(i == 0, bexp_ref[i] != prev)

        @pl.when(changed)
        def _():
            wg_b[...] = wg_ref[0].astype(BF16)
            wu_b[...] = wu_ref[0].astype(BF16)
            wd_b[...] = wd_ref[0].astype(BF16)

        w = x_ref[...]
        lo = lax.bitcast_convert_type(w << 16, F32).astype(BF16)
        hi = lax.bitcast_convert_type(w & jnp.uint32(0xFFFF0000), F32).astype(BF16)
        gate = _dot(lo, wg_b[:PACKED, :]) + _dot(hi, wg_b[PACKED:, :])
        up = _dot(lo, wu_b[:PACKED, :]) + _dot(hi, wu_b[PACKED:, :])
        hid = jax.nn.silu(gate) * up
        y_ref[...] = _dot(hid.astype(BF16), wd_b[...])

    @pl.when(jnp.logical_not(active))
    def _():
        y_ref[...] = jnp.zeros_like(y_ref)


def _experts(block_expert, n_active, xs, w_gate, w_up, w_down):
    n_rows = xs.shape[0]
    rows = EXPERT_ROWS
    n_blocks = n_rows // rows

    def live(i, nact):
        return jnp.minimum(i, nact[0] - 1)

    grid_spec = pltpu.PrefetchScalarGridSpec(
        num_scalar_prefetch=2,
        grid=(n_blocks,),
        in_specs=[
            pl.BlockSpec((rows, PACKED), lambda i, be, na: (live(i, na), 0)),
            pl.BlockSpec((1, D_MODEL, EXPERT_DIM), lambda i, be, na: (be[live(i, na)], 0, 0)),
            pl.BlockSpec((1, D_MODEL, EXPERT_DIM), lambda i, be, na: (be[live(i, na)], 0, 0)),
            pl.BlockSpec((1, EXPERT_DIM, D_MODEL), lambda i, be, na: (be[live(i, na)], 0, 0)),
        ],
        out_specs=pl.BlockSpec((rows, D_MODEL), lambda i, be, na: (i, 0)),
        scratch_shapes=[
            pltpu.VMEM((D_MODEL, EXPERT_DIM), BF16),
            pltpu.VMEM((D_MODEL, EXPERT_DIM), BF16),
            pltpu.VMEM((EXPERT_DIM, D_MODEL), BF16),
        ],
    )
    return pl.pallas_call(
        _experts_kernel,
        grid_spec=grid_spec,
        out_shape=jax.ShapeDtypeStruct((n_rows, D_MODEL), F32),
        compiler_params=pltpu.CompilerParams(
            dimension_semantics=("arbitrary",), vmem_limit_bytes=VMEM_LIMIT_BYTES),
        name="experts",
    )(block_expert, n_active, xs, w_gate, w_up, w_down)


def _combine_kernel(pstart_ref, eidx_ref, rank_ref, base_ref, gate_ref, g2_ref, b2_ref, ys_ref, o_ref, buf, sem):
    rows = base_ref.shape[0]

    def row_copy(k, t, src):
        return pltpu.make_async_copy(ys_ref.at[pl.ds(src, 1)], buf.at[k, pl.ds(t, 1)], sem)

    def issue(t, c):
        for k in range(TOP_K):
            src = pstart_ref[eidx_ref[k, t]] + rank_ref[k, t]
            row_copy(k, t, src).start(priority=k % 2)
        return c

    lax.fori_loop(0, rows, issue, 0, unroll=4)
    for k in range(TOP_K):
        pltpu.make_async_copy(ys_ref.at[pl.ds(0, rows)], buf.at[k], sem).wait()

    acc = base_ref[...]
    gates = gate_ref[...]
    for k in range(TOP_K):
        acc = acc + buf[k] * gates[:, k:k + 1]
    o_ref[...] = _layer_norm(acc, g2_ref[...], b2_ref[...])


def _combine(pad_start, eidx, rank, base, gates_t, g2, b2, ys):
    t = base.shape[0]
    rows = COMBINE_ROWS
    smem_tile = pl.BlockSpec((TOP_K, rows), lambda i, *_: (0, i), memory_space=pltpu.SMEM)
    grid_spec = pltpu.PrefetchScalarGridSpec(
        num_scalar_prefetch=1,
        grid=(t // rows,),
        in_specs=[
            smem_tile, smem_tile,
            pl.BlockSpec((rows, D_MODEL), lambda i, *_: (i, 0)),
            pl.BlockSpec((rows, TOP_K), lambda i, *_: (i, 0)),
            pl.BlockSpec((1, D_MODEL), lambda i, *_: (0, 0)),
            pl.BlockSpec((1, D_MODEL), lambda i, *_: (0, 0)),
            pl.BlockSpec(memory_space=pl.ANY),
        ],
        out_specs=pl.BlockSpec((rows, D_MODEL), lambda i, *_: (i, 0)),
        scratch_shapes=[
            pltpu.VMEM((TOP_K, rows, D_MODEL), F32),
            pltpu.SemaphoreType.DMA(()),
        ],
    )
    return pl.pallas_call(
        _combine_kernel,
        grid_spec=grid_spec,
        out_shape=jax.ShapeDtypeStruct((t, D_MODEL), F32),
        compiler_params=pltpu.CompilerParams(
            dimension_semantics=("arbitrary",), vmem_limit_bytes=VMEM_LIMIT_BYTES),
        name="combine",
    )(pad_start, eidx, rank, base, gates_t, g2, b2, ys)


def _rope_tables(seq):
    half = ROT_DIM // 2
    inv_freq = ROPE_THETA ** (-jnp.arange(half, dtype=F32) / half)
    ang = jnp.arange(seq).astype(F32)[:, None] * inv_freq[None, :]
    cos, sin = jnp.cos(ang), jnp.sin(ang)
    rest = HEAD_DIM - ROT_DIM
    ones = jnp.ones((seq, rest), F32)
    zeros = jnp.zeros((seq, rest), F32)
    zh = jnp.zeros((seq, half), F32)
    cos_h = jnp.concatenate([cos, cos, ones], axis=1)
    sneg_h = jnp.concatenate([-sin, zh, zeros], axis=1)
    spos_h = jnp.concatenate([zh, sin, zeros], axis=1)
    rep = LANES // HEAD_DIM
    return jnp.tile(cos_h, (1, rep)), jnp.tile(sneg_h, (1, rep)), jnp.tile(spos_h, (1, rep))


def _layer(alpha, x, w_in, w_pool, pool_scale, attn_sinks, w_o, ln1_g, ln1_b, w_router, router_bias,
           w_exp_gate, w_exp_up, w_exp_down, w_sh_gate, w_sh_up, w_sh_down, ln2_g, ln2_b):
    batch, seq, _ = x.shape
    t = batch * seq
    x2 = x.reshape(t, D_MODEL)

    c2 = POOL_WIDTH + ATTN_WIDTH
    kv_cols = []
    for base_col in (c2, c2 + KV_WIDTH):
        for h in range(N_KV_HEADS):
            w_h = w_in[:, base_col + h * HEAD_DIM:base_col + (h + 1) * HEAD_DIM]
            kv_cols += [w_h, w_h]
    w_ext = jnp.concatenate([w_in[:, :c2]] + kv_cols, axis=1).astype(BF16)
    cos_t, sneg_t, spos_t = _rope_tables(seq)

    u, q, kd, vd = _in_proj(x2, w_ext, cos_t, sneg_t, spos_t, seq)
    mixin = _mixer(u, q, kd, vd, w_pool.astype(BF16), pool_scale.reshape(1, POOL_WIDTH),
                   attn_sinks.astype(F32), batch, seq)

    r = lax.broadcasted_iota(I32, (POST_ROWS, POST_ROWS), 0)
    c = lax.broadcasted_iota(I32, (POST_ROWS, POST_ROWS), 1)
    tri = (r < c).astype(BF16)
    base, xpk, eidx, gates, rank, counts = _post_mix(
        alpha, mixin, x2, w_o.astype(BF16), ln1_g.reshape(1, D_MODEL), ln1_b.reshape(1, D_MODEL),
        w_sh_gate.astype(BF16), w_sh_up.astype(BF16), w_sh_down.astype(BF16),
        w_router.T.astype(BF16), router_bias.reshape(N_EXPERTS, 1).astype(F32), tri)

    n_blocks = (t * TOP_K + N_EXPERTS * (EXPERT_ROWS - 1)) // EXPERT_ROWS
    counts_i = counts[:, 0].astype(I32)
    padded = (counts_i + EXPERT_ROWS - 1) // EXPERT_ROWS * EXPERT_ROWS
    pad_end = jnp.cumsum(padded)
    pad_start = (pad_end - padded).astype(I32)
    n_active = (pad_end[-1:] // EXPERT_ROWS).astype(I32)
    block_expert = jnp.minimum(
        jnp.searchsorted(pad_end, jnp.arange(n_blocks, dtype=I32) * EXPERT_ROWS, side="right"),
        N_EXPERTS - 1).astype(I32)

    xs = _dispatch(pad_start, counts_i, n_active, eidx, rank, xpk, n_blocks * EXPERT_ROWS)
    ys = _experts(block_expert, n_active, xs, w_exp_gate, w_exp_up, w_exp_down)
    out = _combine(pad_start, eidx, rank, base, gates.T, ln2_g.reshape(1, D_MODEL),
                   ln2_b.reshape(1, D_MODEL), ys)
    return out.reshape(batch, seq, D_MODEL)


def kernel(x, w_in, w_pool, pool_scale, attn_sinks, w_o, ln1_g, ln1_b, w_router, router_bias,
           w_exp_gate, w_exp_up, w_exp_down, w_sh_gate, w_sh_up, w_sh_down, ln2_g, ln2_b):
    depth = w_in.shape[0]
    alpha = (2.0 * depth) ** 0.25
    for l in range(depth):
        x = _layer(alpha, x, w_in[l], w_pool[l], pool_scale[l], attn_sinks[l], w_o[l], ln1_g[l], ln1_b[l],
                   w_router[l], router_bias[l], w_exp_gate[l], w_exp_up[l], w_exp_down[l],
                   w_sh_gate[l], w_sh_up[l], w_sh_down[l], ln2_g[l], ln2_b[l])
    return x
```
